```python
import math
import jax, jax.numpy as jnp
from jax import lax
import numpy as np

D_MODEL = 1024
BATCH = 8
SEQ = 4096
DEPTH = 2

HEAD_DIM = 64
N_HEADS = D_MODEL // HEAD_DIM
SB_HEADS = N_HEADS // 4
DIL_HEADS = N_HEADS - SB_HEADS
DIL_PATTERNS = ((128, 1), (512, 4), (2048, 16))
DIL_GROUP_HEADS = DIL_HEADS // len(DIL_PATTERNS)
OUT_WIDTH = (SB_HEADS + DIL_GROUP_HEADS) * HEAD_DIM
BLOCK = 128
N_BUCKETS = 32
MAX_DISTANCE = 2048
D_FF = 2816
RWKV_HEAD = 64
RWKV_HEADS = D_MODEL // RWKV_HEAD
D_DECAY_LORA = 64
D_AAA_LORA = 64
D_GATE_LORA = 160
NORM_EPS = 1e-6
GN_EPS = 64e-5
NEG_INF = -1e30
N_EVEN = (DEPTH + 1) // 2
N_ODD = DEPTH // 2

kernel_name = "hybrid_stickbreak_dilated_rwkv7_macaron"


def rms_norm(x, g, eps=NORM_EPS):
    x32 = x.astype(jnp.float32)
    y = x32 * lax.rsqrt(jnp.mean(x32 * x32, axis=-1, keepdims=True) + eps)
    return (y * g.astype(jnp.float32)).astype(x.dtype)


def swiglu(h, w_gate, w_up, w_down):
    return (jax.nn.silu(h @ w_gate) * (h @ w_up)) @ w_down


def t5_bucket(dist):
    max_exact = N_BUCKETS // 2
    d = jnp.maximum(dist, 1).astype(jnp.float32)
    large = max_exact + (jnp.log(d / max_exact) / math.log(MAX_DISTANCE / max_exact)
                         * (N_BUCKETS - max_exact)).astype(jnp.int32)
    large = jnp.minimum(large, N_BUCKETS - 1)
    return jnp.where(dist < max_exact, dist, large)


def stick_breaking_attention(q, k, v):
    B, S, H, Dh = q.shape
    nb = S // BLOCK
    scale = Dh ** -0.5
    qb = q.reshape(B, nb, BLOCK, H, Dh).transpose(1, 0, 3, 2, 4)
    kpos = jnp.arange(S)

    def one_block(args):
        qi, n = args
        z = jnp.einsum('bhqd,bshd->bhqs', qi, k).astype(jnp.float32) * scale
        qpos = n * BLOCK + jnp.arange(BLOCK)
        strict = kpos[None, :] < qpos[:, None]
        log_keep = jnp.where(strict, jax.nn.log_sigmoid(-z), 0.0)
        after = lax.cumsum(log_keep, axis=3, reverse=True) - log_keep
        weight = jnp.where(strict, jnp.exp(jax.nn.log_sigmoid(z) + after), 0.0)
        return jnp.einsum('bhqs,bshd->bqhd', weight.astype(v.dtype), v)

    out = lax.map(one_block, (qb, jnp.arange(nb)))
    return out.transpose(1, 0, 2, 3, 4).reshape(B, S, H, Dh)


def dilated_window_attention(q, k, v, bias_table, window, dilation):
    B, S, G, Dh = q.shape
    r = dilation
    L = S // r
    span = window // r
    nb = -(-L // BLOCK)
    Lp = nb * BLOCK

    def to_sub(t):
        t = t.reshape(B, L, r, G, Dh).transpose(0, 2, 1, 3, 4)
        t = jnp.pad(t, ((0, 0), (0, 0), (0, Lp - L), (0, 0), (0, 0)))
        return t.reshape(B, r, nb, BLOCK, G, Dh)

    def with_prev(t):
        prev = jnp.pad(t, ((0, 0), (0, 0), (1, 0), (0, 0), (0, 0), (0, 0)))[:, :, :-1]
        return jnp.concatenate([prev, t], axis=3)

    qs = to_sub(q)
    kw, vw = with_prev(to_sub(k)), with_prev(to_sub(v))
    logits = jnp.einsum('brnqgd,brnkgd->brngqk', qs, kw).astype(jnp.float32) * (Dh ** -0.5)
    qi = jnp.arange(BLOCK)[:, None]
    kj = jnp.arange(2 * BLOCK)[None, :] - BLOCK
    dist = qi - kj
    in_window = (dist >= 0) & (dist <= span)
    has_key = (jnp.arange(nb)[:, None, None] > 0) | (kj[None] >= 0)
    mask = (in_window[None] & has_key)[:, None]
    bias = bias_table[t5_bucket(jnp.maximum(dist, 0) * r)].astype(jnp.float32).transpose(2, 0, 1)
    logits = jnp.where(mask, logits + bias, NEG_INF)
    m = jnp.max(logits, axis=-1, keepdims=True)
    p = jnp.exp(logits - m)
    den = jnp.sum(p, axis=-1, keepdims=True)
    o = jnp.einsum('brngqk,brnkgd->brnqgd', (p / den).astype(v.dtype), vw)
    lse = (m + jnp.log(den))[..., 0].transpose(0, 1, 2, 4, 3)

    def from_sub(t):
        t = t.reshape((B, r, Lp) + t.shape[4:])[:, :, :L]
        return jnp.swapaxes(t, 1, 2).reshape((B, S) + t.shape[3:])

    return from_sub(o), from_sub(lse)


def parallel_attention_mixer(h, w_in, q_norm, k_norm, w_out, rel_bias):
    B, S, _ = h.shape
    proj = h @ w_in
    a_cols = 3 * SB_HEADS * HEAD_DIM
    sb = proj[..., :a_cols].reshape(B, S, 3, SB_HEADS, HEAD_DIM)
    dl = proj[..., a_cols:].reshape(B, S, 3, DIL_HEADS, HEAD_DIM)
    out_a = stick_breaking_attention(sb[:, :, 0], sb[:, :, 1], sb[:, :, 2])
    q = rms_norm(dl[:, :, 0], q_norm)
    k = rms_norm(dl[:, :, 1], k_norm)
    v = dl[:, :, 2]
    outs, lses = [], []
    for g, (window, dilation) in enumerate(DIL_PATTERNS):
        sl = slice(g * DIL_GROUP_HEADS, (g + 1) * DIL_GROUP_HEADS)
        o, l = dilated_window_attention(q[:, :, sl], k[:, :, sl], v[:, :, sl],
                                        rel_bias[:, sl], window, dilation)
        outs.append(o)
        lses.append(l)
    alpha = jax.nn.softmax(jnp.stack(lses, axis=0), axis=0)
    out_b = jnp.sum(alpha[..., None].astype(v.dtype) * jnp.stack(outs, axis=0), axis=0)
    merged = jnp.concatenate([out_a.reshape(B, S, -1), out_b.reshape(B, S, -1)], axis=-1)
    return merged @ w_out


def rwkv7_step(state, inp):
    r_t, w_t, k_t, v_t, a_t, b_t = inp
    sa = jnp.einsum('bhvk,bhk->bhv', state, a_t)
    state = (state * w_t[:, :, None, :] + sa[..., None] * b_t[:, :, None, :]
             + v_t[..., None] * k_t[:, :, None, :])
    return state, jnp.einsum('bhvk,bhk->bhv', state, r_t)


def rwkv7_time_mix(h, mix, w0, w1, w2, a0, a1, a2, g1, g2, k_k, k_a, r_k,
                   w_r, w_k, w_v, w_o, lnx_g, lnx_b):
    B, S, D = h.shape
    H, N = RWKV_HEADS, RWKV_HEAD
    f32 = jnp.float32
    xx = jnp.pad(h, ((0, 0), (1, 0), (0, 0)))[:, :-1] - h
    xr, xw, xk, xv, xa, xg = [h + xx * mix[i] for i in range(6)]
    r = (xr @ w_r).astype(f32)
    k = (xk @ w_k).astype(f32)
    v = (xv @ w_v).astype(f32)
    w_log = -jax.nn.softplus(-(w0 + jnp.tanh(xw @ w1) @ w2).astype(f32)) - 0.5
    decay = jnp.exp(-jnp.exp(w_log))
    a = jax.nn.sigmoid((a0 + (xa @ a1) @ a2).astype(f32))
    g = jax.nn.sigmoid(xg @ g1) @ g2
    heads = lambda t: t.reshape(B, S, H, N)
    kk = heads(k * k_k.astype(f32))
    kk = kk / jnp.maximum(jnp.sqrt(jnp.sum(kk * kk, axis=-1, keepdims=True)), 1e-12)
    k = k * (1.0 + (a - 1.0) * k_a.astype(f32))
    r_h, k_h, v_h, a_h, w_h = heads(r), heads(k), heads(v), heads(a), heads(decay)
    tm = lambda t: jnp.swapaxes(t, 0, 1)
    seqs = (tm(r_h), tm(w_h), tm(k_h), tm(v_h), tm(-kk), tm(kk * a_h))
    _, y = lax.scan(rwkv7_step, jnp.zeros((B, H, N, N), f32), seqs)
    y = jnp.swapaxes(y, 0, 1)
    mu = jnp.mean(y, axis=-1, keepdims=True)
    var = jnp.mean(jnp.square(y - mu), axis=-1, keepdims=True)
    y = ((y - mu) * lax.rsqrt(var + GN_EPS)).reshape(B, S, D)
    y = y * lnx_g.astype(f32) + lnx_b.astype(f32)
    y = y + (jnp.sum(r_h * k_h * r_k.astype(f32), axis=-1, keepdims=True) * v_h).reshape(B, S, D)
    return (y.astype(h.dtype) * g) @ w_o


def setup_inputs(seed: int = 0) -> dict:
    key = jax.random.key(seed)
    ks = iter(jax.random.split(key, 48))
    D = D_MODEL
    nrm = lambda shape, scale: jax.random.normal(next(ks), shape, jnp.float32) * scale
    uni = lambda shape, lo, hi: jax.random.uniform(next(ks), shape, jnp.float32, lo, hi)
    return {
        "x": nrm((BATCH, SEQ, D), 1.0),
        "ffn_norm": 1.0 + nrm((DEPTH, 2, D), 0.05),
        "ffn_w_gate": nrm((DEPTH, 2, D, D_FF), D ** -0.5),
        "ffn_w_up": nrm((DEPTH, 2, D, D_FF), D ** -0.5),
        "ffn_w_down": nrm((DEPTH, 2, D_FF, D), D_FF ** -0.5),
        "mix_norm": 1.0 + nrm((DEPTH, D), 0.05),
        "rel_bias": nrm((N_BUCKETS, DIL_HEADS), 0.3),
        "attn_w_in": nrm((N_EVEN, D, 3 * N_HEADS * HEAD_DIM), D ** -0.5),
        "attn_q_norm": 1.0 + nrm((N_EVEN, HEAD_DIM), 0.05),
        "attn_k_norm": 1.0 + nrm((N_EVEN, HEAD_DIM), 0.05),
        "attn_w_out": nrm((N_EVEN, OUT_WIDTH, D), OUT_WIDTH ** -0.5),
        "rw_mix": uni((N_ODD, 6, D), 0.0, 1.0),
        "rw_w0": uni((N_ODD, D), -4.0, 0.0),
        "rw_w1": nrm((N_ODD, D, D_DECAY_LORA), D ** -0.5),
        "rw_w2": nrm((N_ODD, D_DECAY_LORA, D), 0.1 * D_DECAY_LORA ** -0.5),
        "rw_a0": nrm((N_ODD, D), 0.1),
        "rw_a1": nrm((N_ODD, D, D_AAA_LORA), D ** -0.5),
        "rw_a2": nrm((N_ODD, D_AAA_LORA, D), 0.1 * D_AAA_LORA ** -0.5),
        "rw_g1": nrm((N_ODD, D, D_GATE_LORA), D ** -0.5),
        "rw_g2": nrm((N_ODD, D_GATE_LORA, D), D_GATE_LORA ** -0.5),
        "rw_kk": 0.85 + nrm((N_ODD, D), 0.05),
        "rw_ka": 1.0 + nrm((N_ODD, D), 0.05),
        "rw_rk": nrm((N_ODD, RWKV_HEADS, RWKV_HEAD), 0.1),
        "rw_wr": nrm((N_ODD, D, D), D ** -0.5),
        "rw_wk": nrm((N_ODD, D, D), D ** -0.5),
        "rw_wv": nrm((N_ODD, D, D), D ** -0.5),
        "rw_wo": nrm((N_ODD, D, D), D ** -0.5),
        "rw_lnx_g": 1.0 + nrm((N_ODD, D), 0.05),
        "rw_lnx_b": nrm((N_ODD, D), 0.01),
    }


def reference(x, ffn_norm, ffn_w_gate, ffn_w_up, ffn_w_down, mix_norm, rel_bias,
              attn_w_in, attn_q_norm, attn_k_norm, attn_w_out,
              rw_mix, rw_w0, rw_w1, rw_w2, rw_a0, rw_a1, rw_a2, rw_g1, rw_g2,
              rw_kk, rw_ka, rw_rk, rw_wr, rw_wk, rw_wv, rw_wo, rw_lnx_g, rw_lnx_b):
    for layer in range(DEPTH):
        x = x + 0.5 * swiglu(rms_norm(x, ffn_norm[layer, 0]), ffn_w_gate[layer, 0],
                             ffn_w_up[layer, 0], ffn_w_down[layer, 0])
        h = rms_norm(x, mix_norm[layer])
        if layer % 2 == 0:
            e = layer // 2
            x = x + parallel_attention_mixer(h, attn_w_in[e], attn_q_norm[e], attn_k_norm[e],
                                             attn_w_out[e], rel_bias)
        else:
            o = layer // 2
            x = x + rwkv7_time_mix(h, rw_mix[o], rw_w0[o], rw_w1[o], rw_w2[o], rw_a0[o],
                                   rw_a1[o], rw_a2[o], rw_g1[o], rw_g2[o], rw_kk[o], rw_ka[o],
                                   rw_rk[o], rw_wr[o], rw_wk[o], rw_wv[o], rw_wo[o],
                                   rw_lnx_g[o], rw_lnx_b[o])
        x = x + 0.5 * swiglu(rms_norm(x, ffn_norm[layer, 1]), ffn_w_gate[layer, 1],
                             ffn_w_up[layer, 1], ffn_w_down[layer, 1])
    return x
```

```python
import functools
import math

import jax
import jax.numpy as jnp
from jax import lax
from jax.experimental import pallas as pl
from jax.experimental.pallas import tpu as pltpu

F32 = jnp.float32
BF16 = jnp.bfloat16

HEAD_DIM = 64
LANES = 128
SB_HEADS = 4
DIL_HEADS = 12
DIL_PATTERNS = ((128, 1), (512, 4), (2048, 16))
DIL_GROUP_HEADS = 4
ATT_BLOCK = 128
N_BUCKETS = 32
MAX_DISTANCE = 2048
NORM_EPS = 1e-6
GN_EPS = 64e-5
NEG_INF = -1e30
SCAN_CHUNK = 64
VMEM_LIMIT = 56 * 1024 * 1024


def _params(*sem):
    return pltpu.CompilerParams(dimension_semantics=sem, vmem_limit_bytes=VMEM_LIMIT)


def _resident(shape):
    nd = len(shape)
    return pl.BlockSpec(shape, lambda *_: (0,) * nd, pipeline_mode=pl.Buffered(1))


def _rms(x, g):
    ms = jnp.mean(x * x, axis=-1, keepdims=True)
    return x * lax.rsqrt(ms + NORM_EPS) * g


def _dot(a, b):
    return jnp.dot(a, b, preferred_element_type=F32)


def _dot_nt(a, b):
    return lax.dot_general(a, b, (((1,), (1,)), ((), ())), preferred_element_type=F32)


def _split_bf16(x):
    hi = x.astype(BF16)
    lo = (x - hi.astype(F32)).astype(BF16)
    return hi, lo


def _head_ones():
    r = lax.broadcasted_iota(jnp.int32, (LANES, LANES), 0) // HEAD_DIM
    c = lax.broadcasted_iota(jnp.int32, (LANES, LANES), 1) // HEAD_DIM
    return (r == c).astype(BF16)


def _head_sum(x, ones, exact):
    outs = []
    for c in range(x.shape[1] // LANES):
        xc = x[:, c * LANES:(c + 1) * LANES]
        if exact:
            hi, lo = _split_bf16(xc)
            outs.append(_dot(hi, ones) + _dot(lo, ones))
        else:
            outs.append(_dot(xc.astype(BF16), ones))
    return outs[0] if len(outs) == 1 else jnp.concatenate(outs, axis=1)


FFN_TM = 512
FFN_FK = 256


def _ffn_kernel(x_ref, g_ref, wg_ref, wu_ref, wd_ref, o_ref):
    x = x_ref[...]
    h = _rms(x, g_ref[...]).astype(BF16)
    d_ff = wg_ref.shape[1]
    acc = jnp.zeros(x.shape, F32)
    for c in range(d_ff // FFN_FK):
        sl = slice(c * FFN_FK, (c + 1) * FFN_FK)
        gate = _dot(h, wg_ref[:, sl])
        up = _dot(h, wu_ref[:, sl])
        act = (gate * jax.nn.sigmoid(gate) * up).astype(BF16)
        acc = acc + _dot(act, wd_ref[sl, :])
    o_ref[...] = x + 0.5 * acc


def _ffn(x2, g, wg, wu, wd):
    m, d = x2.shape
    tm = min(FFN_TM, m)
    return pl.pallas_call(
        _ffn_kernel,
        out_shape=jax.ShapeDtypeStruct((m, d), F32),
        grid=(m // tm,),
        in_specs=[
            pl.BlockSpec((tm, d), lambda i: (i, 0)),
            _resident((1, d)),
            _resident(wg.shape),
            _resident(wu.shape),
            _resident(wd.shape),
        ],
        out_specs=pl.BlockSpec((tm, d), lambda i: (i, 0)),
        compiler_params=_params("parallel"),
        name="ffn",
    )(x2, g.reshape(1, d), wg, wu, wd)


PROJ_TM = 512
PROJ_NC = 256
SB_COLS = 3 * SB_HEADS * HEAD_DIM
DIL_COLS = DIL_HEADS * HEAD_DIM


def _proj_kernel(x_ref, g_ref, w_ref, qn_ref, kn_ref, o_ref):
    h = _rms(x_ref[...], g_ref[...]).astype(BF16)
    ones = _head_ones()
    scale = HEAD_DIM ** -0.5
    q_lo, k_lo, v_lo = SB_COLS, SB_COLS + DIL_COLS, SB_COLS + 2 * DIL_COLS
    for c in range(w_ref.shape[1] // PROJ_NC):
        lo = c * PROJ_NC
        sl = slice(lo, lo + PROJ_NC)
        p = _dot(h, w_ref[:, sl])
        if q_lo <= lo < v_lo:
            gain = qn_ref[...] if lo < k_lo else kn_ref[...]
            ms = _head_sum(p * p, ones, exact=False) * (1.0 / HEAD_DIM)
            p = p * lax.rsqrt(ms + NORM_EPS) * gain
        if lo < SB_HEADS * HEAD_DIM or q_lo <= lo < k_lo:
            p = p * scale
        o_ref[:, sl] = p.astype(BF16)


def _proj(x2, g, w_in, q_norm, k_norm):
    m, d = x2.shape
    n = w_in.shape[1]
    tm = min(PROJ_TM, m)
    tile = lambda v: jnp.tile(v.astype(F32), PROJ_NC // HEAD_DIM).reshape(1, PROJ_NC)
    return pl.pallas_call(
        _proj_kernel,
        out_shape=jax.ShapeDtypeStruct((m, n), BF16),
        grid=(m // tm,),
        in_specs=[
            pl.BlockSpec((tm, d), lambda i: (i, 0)),
            _resident((1, d)),
            _resident(w_in.shape),
            _resident((1, PROJ_NC)),
            _resident((1, PROJ_NC)),
        ],
        out_specs=pl.BlockSpec((tm, n), lambda i: (i, 0)),
        compiler_params=_params("parallel"),
        name="attn_proj",
    )(x2, g.reshape(1, d), w_in, tile(q_norm), tile(k_norm))


def _sb_kernel(q_ref, k_ref, v_ref, o_ref):
    qi = pl.program_id(2)
    tq = q_ref.shape[0]
    tk = ATT_BLOCK
    q = q_ref[...]
    lane = lax.broadcasted_iota(jnp.int32, (1, LANES), 1)
    head0 = lane < HEAD_DIM
    zero = jnp.zeros_like(q)
    qh = (jnp.where(head0, q, zero), jnp.where(head0, zero, q))
    row = lax.broadcasted_iota(jnp.int32, (tq, tk), 0)
    col = lax.broadcasted_iota(jnp.int32, (tq, tk), 1)
    strict = col < row
    later = (lax.broadcasted_iota(jnp.int32, (tk, tk), 0)
             > lax.broadcasted_iota(jnp.int32, (tk, tk), 1)).astype(BF16)

    def block(j, carry, masked):
        start = pl.multiple_of(j * tk, tk)
        kb = k_ref[pl.ds(start, tk), :]
        vb = v_ref[pl.ds(start, tk), :]
        out = []
        for h in range(2):
            run, acc = carry[h]
            z = _dot_nt(qh[h], kb)
            log_keep = jnp.minimum(-z, 0.0) - jnp.log(1.0 + jnp.exp(-jnp.abs(z)))
            if masked:
                log_keep = jnp.where(strict, log_keep, 0.0)
            hi, lo = _split_bf16(log_keep)
            after = _dot(hi, later) + _dot(lo, later)
            w = jnp.exp(z + log_keep + after + run)
            if masked:
                w = jnp.where(strict, w, 0.0)
            acc = acc + _dot(w.astype(BF16), vb)
            run = run + jnp.sum(log_keep, axis=1, keepdims=True)
            out.append((run, acc))
        return tuple(out)

    init = tuple((jnp.zeros((tq, 1), F32), jnp.zeros((tq, LANES), F32)) for _ in range(2))
    carry = block(qi, init, True)
    carry = lax.fori_loop(0, qi, lambda i, c: block(qi - 1 - i, c, False), carry)
    o_ref[...] = jnp.where(head0, carry[0][1], carry[1][1]).astype(o_ref.dtype)


def _sb_attention(qkv):
    b, s, _ = qkv.shape
    pairs = SB_HEADS * HEAD_DIM // LANES
    tq = ATT_BLOCK
    return pl.pallas_call(
        _sb_kernel,
        out_shape=jax.ShapeDtypeStruct((b, s, SB_HEADS * HEAD_DIM), BF16),
        grid=(b, pairs, s // tq),
        in_specs=[
            pl.BlockSpec((None, tq, LANES), lambda bi, p, i: (bi, i, p)),
            pl.BlockSpec((None, s, LANES), lambda bi, p, i: (bi, 0, pairs + p)),
            pl.BlockSpec((None, s, LANES), lambda bi, p, i: (bi, 0, 2 * pairs + p)),
        ],
        out_specs=pl.BlockSpec((None, tq, LANES), lambda bi, p, i: (bi, i, p)),
        compiler_params=_params("parallel", "parallel", "arbitrary"),
        name="sb_attn",
    )(qkv, qkv, qkv)


def _t5_bucket(dist):
    max_exact = N_BUCKETS // 2
    d = jnp.maximum(dist, 1).astype(F32)
    large = max_exact + (jnp.log(d / max_exact) / math.log(MAX_DISTANCE / max_exact)
                         * (N_BUCKETS - max_exact)).astype(jnp.int32)
    large = jnp.minimum(large, N_BUCKETS - 1)
    return jnp.where(dist < max_exact, dist, large)


def _bias_blocks(rel_bias_group, dilation):
    qi = jnp.arange(ATT_BLOCK)[:, None]
    kj = jnp.arange(2 * ATT_BLOCK)[None, :] - ATT_BLOCK
    dist = qi - kj
    bias = rel_bias_group[_t5_bucket(jnp.maximum(dist, 0) * dilation)].astype(F32)
    return bias.transpose(2, 0, 1).reshape(2, 2, ATT_BLOCK, 2 * ATT_BLOCK)


def _dil_kernel(q_ref, kp_ref, kc_ref, vp_ref, vc_ref, bias_ref, o_ref, lse_ref):
    n = pl.program_id(3)
    blk = ATT_BLOCK
    q = q_ref[...]
    lane = lax.broadcasted_iota(jnp.int32, (1, LANES), 1)
    head0 = lane < HEAD_DIM
    zero = jnp.zeros_like(q)
    qh = (jnp.where(head0, q, zero), jnp.where(head0, zero, q))
    row = lax.broadcasted_iota(jnp.int32, (blk, 2 * blk), 0)
    col = lax.broadcasted_iota(jnp.int32, (blk, 2 * blk), 1)
    dist = row - col + blk
    mask = (dist >= 0) & (dist <= blk) & ((n > 0) | (col >= blk))
    kp, kc, vp, vc = kp_ref[...], kc_ref[...], vp_ref[...], vc_ref[...]
    outs, lses = [], []
    for h in range(2):
        z = jnp.concatenate([_dot_nt(qh[h], kp), _dot_nt(qh[h], kc)], axis=1)
        logits = jnp.where(mask, z + bias_ref[h], NEG_INF)
        m = jnp.max(logits, axis=1, keepdims=True)
        p = jnp.exp(logits - m)
        den = jnp.sum(p, axis=1, keepdims=True)
        pn = (p / den).astype(BF16)
        outs.append(_dot(pn[:, :blk], vp) + _dot(pn[:, blk:], vc))
        lses.append(m + jnp.log(den))
    o_ref[...] = jnp.where(head0, outs[0], outs[1])
    lse_ref[...] = jnp.where(head0, lses[0], lses[1])


def _dil_attention(q, k, v, col_blocks, bias):
    b, r, l, _ = q.shape
    pairs = DIL_GROUP_HEADS * HEAD_DIM // LANES
    blk = ATT_BLOCK
    qo, ko, vo = col_blocks
    cur = lambda off: (lambda bi, c, p, n: (bi, c, n, off + p))
    prev = lambda off: (lambda bi, c, p, n: (bi, c, jnp.maximum(n - 1, 0), off + p))
    tile = (None, None, blk, LANES)
    out = jax.ShapeDtypeStruct((b, r, l, pairs * LANES), F32)
    return pl.pallas_call(
        _dil_kernel,
        out_shape=(out, out),
        grid=(b, r, pairs, l // blk),
        in_specs=[
            pl.BlockSpec(tile, cur(qo)),
            pl.BlockSpec(tile, prev(ko)),
            pl.BlockSpec(tile, cur(ko)),
            pl.BlockSpec(tile, prev(vo)),
            pl.BlockSpec(tile, cur(vo)),
            pl.BlockSpec((None, 2, blk, 2 * blk), lambda bi, c, p, n: (p, 0, 0, 0)),
        ],
        out_specs=(pl.BlockSpec(tile, cur(0)), pl.BlockSpec(tile, cur(0))),
        compiler_params=_params("parallel", "parallel", "parallel", "arbitrary"),
        name="dil_attn",
    )(q, k, k, v, v, bias)


OUT_TM = 512


def _attn_out_kernel(x_ref, sb_ref, o0_ref, o1_ref, o2_ref, l0_ref, l1_ref, l2_ref, w_ref, y_ref):
    l0, l1, l2 = l0_ref[...], l1_ref[...], l2_ref[...]
    mx = jnp.maximum(jnp.maximum(l0, l1), l2)
    e0, e1, e2 = jnp.exp(l0 - mx), jnp.exp(l1 - mx), jnp.exp(l2 - mx)
    den = e0 + e1 + e2
    out_b = (e0 / den) * o0_ref[...] + (e1 / den) * o1_ref[...] + (e2 / den) * o2_ref[...]
    na = sb_ref.shape[1]
    y = _dot(sb_ref[...], w_ref[:na, :]) + _dot(out_b.astype(BF16), w_ref[na:, :])
    y_ref[...] = x_ref[...] + y


def _attn_out(x2, sb, outs, lses, w_out):
    m, d = x2.shape
    tm = min(OUT_TM, m)
    wide = sb.shape[1]
    row = lambda w: pl.BlockSpec((tm, w), lambda i: (i, 0))
    return pl.pallas_call(
        _attn_out_kernel,
        out_shape=jax.ShapeDtypeStruct((m, d), F32),
        grid=(m // tm,),
        in_specs=[row(d), row(wide)] + [row(wide)] * 6 + [_resident(w_out.shape)],
        out_specs=row(d),
        compiler_params=_params("parallel"),
        name="attn_out",
    )(x2, sb, *outs, *lses, w_out)


RW_TS = 256


def _rwkv_in_kernel(x_ref, xp_ref, g_ref, mix_ref, vec_ref, wr_ref, wk_ref, wv_ref,
                    w1_ref, w2_ref, a1_ref, a2_ref, g1_ref, g2_ref,
                    r_ref, ld_ref, k_ref, v_ref, na_ref, bb_ref, gate_ref, bonus_ref):
    si = pl.program_id(1)
    gain = g_ref[...]
    h = _rms(x_ref[...], gain)
    prev_last = _rms(xp_ref[...], gain)[7:8, :] * (si > 0).astype(F32)
    rows = lax.broadcasted_iota(jnp.int32, h.shape, 0)
    shifted = jnp.where(rows == 0, prev_last, pltpu.roll(h, 1, axis=0))
    xx = shifted - h
    mixed = lambda i: (h + xx * mix_ref[i:i + 1, :]).astype(BF16)
    w0, a0, k_k, k_a, r_k = (vec_ref[i:i + 1, :] for i in range(5))

    r = _dot(mixed(0), wr_ref[...])
    k = _dot(mixed(2), wk_ref[...])
    v = _dot(mixed(3), wv_ref[...])
    lora_w = _dot(jnp.tanh(_dot(mixed(1), w1_ref[...])).astype(BF16), w2_ref[...])
    lora_a = _dot(_dot(mixed(4), a1_ref[...]).astype(BF16), a2_ref[...])
    gate = _dot(jax.nn.sigmoid(_dot(mixed(5), g1_ref[...])).astype(BF16), g2_ref[...])

    t = -(w0 + lora_w)
    w_log = -(jnp.maximum(t, 0.0) + jnp.log(1.0 + jnp.exp(-jnp.abs(t)))) - 0.5
    a = jax.nn.sigmoid(a0 + lora_a)
    ones = _head_ones()
    kk = k * k_k
    norm = jnp.sqrt(_head_sum(kk * kk, ones, exact=False))
    kk = kk / jnp.maximum(norm, 1e-12)
    k = k * (1.0 + (a - 1.0) * k_a)

    r_ref[...] = r
    ld_ref[...] = -jnp.exp(w_log)
    k_ref[...] = k
    v_ref[...] = v
    na_ref[...] = -kk
    bb_ref[...] = kk * a
    gate_ref[...] = gate
    bonus_ref[...] = _head_sum(r * k * r_k, ones, exact=True) * v


def _rwkv_in(x3, g, mix, vecs, wr, wk, wv, w1, w2, a1, a2, g1, g2):
    b, s, d = x3.shape
    ts = min(RW_TS, s)
    tile = pl.BlockSpec((None, ts, d), lambda bi, si: (bi, si, 0))
    prev = pl.BlockSpec((None, 8, d), lambda bi, si: (bi, jnp.maximum(si * (ts // 8) - 1, 0), 0))
    out = jax.ShapeDtypeStruct((b, s, d), F32)
    weights = (wr, wk, wv, w1, w2, a1, a2, g1, g2)
    return pl.pallas_call(
        _rwkv_in_kernel,
        out_shape=(out,) * 8,
        grid=(b, s // ts),
        in_specs=[tile, prev, _resident((1, d)), _resident(mix.shape), _resident(vecs.shape)]
        + [_resident(w.shape) for w in weights],
        out_specs=(tile,) * 8,
        compiler_params=_params("parallel", "arbitrary"),
        name="rwkv_in",
    )(x3, x3, g.reshape(1, d), mix, vecs, *weights)


SCAN_ROWS = 512


def _scan_kernel(r_ref, ld_ref, k_ref, v_ref, na_ref, bb_ref, y_ref, state_ref):
    c_len = SCAN_CHUNK
    two = 2 * c_len

    @pl.when(pl.program_id(2) == 0)
    def _():
        state_ref[...] = jnp.zeros_like(state_ref)

    lane = lax.broadcasted_iota(jnp.int32, (1, LANES), 1)
    head0 = lane < HEAD_DIM
    ri = lax.broadcasted_iota(jnp.int32, (two, two), 0)
    ci = lax.broadcasted_iota(jnp.int32, (two, two), 1)
    same_head = (ri // c_len) == (ci // c_len)
    strict = same_head & ((ri % c_len) > (ci % c_len))
    incl = same_head & ((ri % c_len) >= (ci % c_len))
    eye = ri == ci
    tri = (lax.broadcasted_iota(jnp.int32, (c_len, c_len), 0)
           >= lax.broadcasted_iota(jnp.int32, (c_len, c_len), 1)).astype(BF16)

    def stack(x):
        return jnp.concatenate([jnp.where(head0, x, 0.0), jnp.where(head0, 0.0, x)], axis=0)

    def dup(x):
        return jnp.concatenate([x, x], axis=0)

    bf = lambda x: x.astype(BF16)

    def chunk(ci_, _):
        start = pl.multiple_of(ci_ * c_len, c_len)
        rows = pl.ds(start, c_len)
        r, ld, k, v = r_ref[rows, :], ld_ref[rows, :], k_ref[rows, :], v_ref[rows, :]
        na, bb = na_ref[rows, :], bb_ref[rows, :]

        hi, lo = _split_bf16(ld)
        cum = _dot(tri, hi) + _dot(tri, lo)
        total = cum[c_len - 1:c_len, :]
        e_neg = jnp.exp(-cum)
        e_tail = jnp.exp(total - cum)
        a_t = na * jnp.exp(cum - ld)
        r_t = r * jnp.exp(cum)
        at2, rt2 = bf(stack(a_t)), stack(r_t)
        bt2, kt2 = bf(dup(bb * e_neg)), bf(dup(k * e_neg))
        v2 = bf(stack(v))
        rt2b = bf(rt2)

        n_ab = jnp.where(strict, _dot_nt(at2, bt2), 0.0)
        a_ak = jnp.where(strict, _dot_nt(at2, kt2), 0.0)
        b_rb = bf(jnp.where(incl, _dot_nt(rt2b, bt2), 0.0))
        b_rk = bf(jnp.where(incl, _dot_nt(rt2b, kt2), 0.0))

        inv = jnp.where(eye, 1.0, n_ab)
        pw = n_ab
        span = 1
        while 2 * span < c_len:
            pwb = bf(pw)
            pw = _dot(pwb, pwb)
            inv = inv + _dot(bf(inv), bf(pw))
            span *= 2
        inv = bf(inv)

        w2 = _dot(inv, at2)
        u3 = _dot(inv, bf(_dot(bf(a_ak), v2)))
        w2b, u3b = bf(w2), bf(u3)
        lhs_t = jnp.transpose(jnp.concatenate([stack(bb * e_tail), stack(k * e_tail)], axis=0))
        lhs_t = bf(lhs_t)
        m_mat = jnp.where(eye, jnp.exp(total), 0.0) + _dot(lhs_t[:, :two], w2b)
        n_mat = _dot(lhs_t, jnp.concatenate([u3b, v2], axis=0))
        q2 = rt2 + _dot(b_rb, w2b)
        y3 = _dot(b_rb, u3b) + _dot(b_rk, v2)

        st = bf(state_ref[...])
        y2 = _dot(bf(q2), st) + y3
        y_ref[rows, :] = y2[:c_len, :] + y2[c_len:, :]
        state_ref[...] = _dot(bf(m_mat), st) + n_mat
        return 0

    lax.fori_loop(0, r_ref.shape[0] // c_len, chunk, 0)


def _scan(r, ld, k, v, na, bb):
    b, s, d = r.shape
    rows = min(SCAN_ROWS, s)
    tile = pl.BlockSpec((None, rows, LANES), lambda bi, p, t: (bi, t, p))
    return pl.pallas_call(
        _scan_kernel,
        out_shape=jax.ShapeDtypeStruct((b, s, d), F32),
        grid=(b, d // LANES, s // rows),
        in_specs=[tile] * 6,
        out_specs=tile,
        scratch_shapes=[pltpu.VMEM((LANES, LANES), F32)],
        compiler_params=_params("parallel", "parallel", "arbitrary"),
        name="rwkv_scan",
    )(r, ld, k, v, na, bb)


RW_OUT_TM = 512


def _rwkv_out_kernel(x_ref, y_ref, bonus_ref, gate_ref, lg_ref, lb_ref, wo_ref, o_ref):
    y = y_ref[...]
    ones = _head_ones()
    mu = _head_sum(y, ones, exact=True) * (1.0 / HEAD_DIM)
    dlt = y - mu
    var = _head_sum(dlt * dlt, ones, exact=True) * (1.0 / HEAD_DIM)
    yn = dlt * lax.rsqrt(var + GN_EPS) * lg_ref[...] + lb_ref[...] + bonus_ref[...]
    o_ref[...] = x_ref[...] + _dot((yn * gate_ref[...]).astype(BF16), wo_ref[...])


def _rwkv_out(x2, y2, bonus2, gate2, lnx_g, lnx_b, wo):
    m, d = x2.shape
    tm = min(RW_OUT_TM, m)
    row = pl.BlockSpec((tm, d), lambda i: (i, 0))
    return pl.pallas_call(
        _rwkv_out_kernel,
        out_shape=jax.ShapeDtypeStruct((m, d), F32),
        grid=(m // tm,),
        in_specs=[row, row, row, row, _resident((1, d)), _resident((1, d)), _resident(wo.shape)],
        out_specs=row,
        compiler_params=_params("parallel"),
        name="rwkv_out",
    )(x2, y2, bonus2, gate2, lnx_g.reshape(1, d), lnx_b.reshape(1, d), wo)


def _attention_layer(x3, mix_g, rel_bias, w_in, q_norm, k_norm, w_out):
    b, s, d = x3.shape
    x2 = x3.reshape(b * s, d)
    qkv = _proj(x2, mix_g, w_in.astype(BF16), q_norm, k_norm).reshape(b, s, -1)
    sb = _sb_attention(qkv)
    outs, lses = [], []
    group_w = DIL_GROUP_HEADS * HEAD_DIM
    tiles = group_w // LANES
    for g, (_, r) in enumerate(DIL_PATTERNS):
        offs = tuple(SB_COLS + part * DIL_COLS + g * group_w for part in range(3))
        bias = _bias_blocks(rel_bias[:, g * DIL_GROUP_HEADS:(g + 1) * DIL_GROUP_HEADS], r)
        if r == 1:
            src = qkv.reshape(b, 1, s, -1)
            o, l = _dil_attention(src, src, src, tuple(o_ // LANES for o_ in offs), bias)
        else:
            sub = lambda lo: (qkv[:, :, lo:lo + group_w].reshape(b, s // r, r, group_w)
                              .transpose(0, 2, 1, 3))
            o, l = _dil_attention(sub(offs[0]), sub(offs[1]), sub(offs[2]), (0, 0, 0), bias)
        back = lambda t: t.transpose(0, 2, 1, 3).reshape(b * s, tiles * LANES)
        outs.append(back(o))
        lses.append(back(l))
    y = _attn_out(x2, sb.reshape(b * s, -1), outs, lses, w_out.astype(BF16))
    return y.reshape(b, s, d)


def _pad_cols(w, n):
    return jnp.pad(w, ((0, 0), (0, n - w.shape[1])))


def _pad_rows(w, n):
    return jnp.pad(w, ((0, n - w.shape[0]), (0, 0)))


def _rwkv_layer(x3, mix_g, mix, w0, w1, w2, a0, a1, a2, g1, g2, k_k, k_a, r_k,
                w_r, w_k, w_v, w_o, lnx_g, lnx_b):
    b, s, d = x3.shape
    bf = lambda w: w.astype(BF16)
    lora = lambda w: -(-w // LANES) * LANES
    vecs = jnp.stack([w0, a0, k_k, k_a, r_k.reshape(d), jnp.zeros_like(w0),
                      jnp.zeros_like(w0), jnp.zeros_like(w0)], axis=0).astype(F32)
    dw, da, dg = lora(w1.shape[1]), lora(a1.shape[1]), lora(g1.shape[1])
    r, ld, k, v, na, bb, gate, bonus = _rwkv_in(
        x3, mix_g, mix, vecs, bf(w_r), bf(w_k), bf(w_v),
        bf(_pad_cols(w1, dw)), bf(_pad_rows(w2, dw)),
        bf(_pad_cols(a1, da)), bf(_pad_rows(a2, da)),
        bf(_pad_cols(g1, dg)), bf(_pad_rows(g2, dg)))
    y = _scan(r, ld, k, v, na, bb)
    flat = lambda t: t.reshape(b * s, d)
    out = _rwkv_out(flat(x3), flat(y), flat(bonus), flat(gate), lnx_g, lnx_b, bf(w_o))
    return out.reshape(b, s, d)


def kernel(x, ffn_norm, ffn_w_gate, ffn_w_up, ffn_w_down, mix_norm, rel_bias, attn_w_in, attn_q_norm, attn_k_norm, attn_w_out, rw_mix, rw_w0, rw_w1, rw_w2, rw_a0, rw_a1, rw_a2, rw_g1, rw_g2, rw_kk, rw_ka, rw_rk, rw_wr, rw_wk, rw_wv, rw_wo, rw_lnx_g, rw_lnx_b):
    b, s, d = x.shape
    depth = ffn_norm.shape[0]

    def ffn(x3, layer, half):
        y = _ffn(x3.reshape(b * s, d), ffn_norm[layer, half],
                 ffn_w_gate[layer, half].astype(BF16), ffn_w_up[layer, half].astype(BF16),
                 ffn_w_down[layer, half].astype(BF16))
        return y.reshape(b, s, d)

    for layer in range(depth):
        x = ffn(x, layer, 0)
        if layer % 2 == 0:
            e = layer // 2
            x = _attention_layer(x, mix_norm[layer], rel_bias, attn_w_in[e], attn_q_norm[e],
                                 attn_k_norm[e], attn_w_out[e])
        else:
            o = layer // 2
            x = _rwkv_layer(x, mix_norm[layer], rw_mix[o], rw_w0[o], rw_w1[o], rw_w2[o],
                            rw_a0[o], rw_a1[o], rw_a2[o], rw_g1[o], rw_g2[o], rw_kk[o],
                            rw_ka[o], rw_rk[o], rw_wr[o], rw_wk[o], rw_wv[o], rw_wo[o],
                            rw_lnx_g[o], rw_lnx_b[o])
        x = ffn(x, layer, 1)
    return x
```

```python
import functools
import math

import jax
import jax.numpy as jnp
from jax import lax
from jax.experimental import pallas as pl
from jax.experimental.pallas import tpu as pltpu

F32 = jnp.float32
BF16 = jnp.bfloat16

HEAD_DIM = 64
LANES = 128
SUBLANES = 8
SB_HEADS = 4
DIL_HEADS = 12
DIL_PATTERNS = ((128, 1), (512, 4), (2048, 16))
DIL_GROUP_HEADS = 4
ATT_BLOCK = 128
N_BUCKETS = 32
MAX_DISTANCE = 2048
NORM_EPS = 1e-6
GN_EPS = 64e-5
NEG_INF = -1e30
LOG2E = math.log2(math.e)
SCAN_CHUNK = 64
VMEM_LIMIT = 56 * 1024 * 1024


def _params(*sem):
    return pltpu.CompilerParams(dimension_semantics=sem, vmem_limit_bytes=VMEM_LIMIT)


def _resident(shape):
    nd = len(shape)
    return pl.BlockSpec(shape, lambda *_: (0,) * nd, pipeline_mode=pl.Buffered(1))


def _rms(x, g):
    ms = jnp.mean(x * x, axis=-1, keepdims=True)
    return x * lax.rsqrt(ms + NORM_EPS) * g


def _dot(a, b):
    return jnp.dot(a, b, preferred_element_type=F32)


def _dot_nt(a, b):
    return lax.dot_general(a, b, (((1,), (1,)), ((), ())), preferred_element_type=F32)


def _split_bf16(x):
    hi = x.astype(BF16)
    lo = (x - hi.astype(F32)).astype(BF16)
    return hi, lo


def _head_ones():
    r = lax.broadcasted_iota(jnp.int32, (LANES, LANES), 0) // HEAD_DIM
    c = lax.broadcasted_iota(jnp.int32, (LANES, LANES), 1) // HEAD_DIM
    return (r == c).astype(BF16)


def _head_sum(x, ones, exact):
    outs = []
    for c in range(x.shape[1] // LANES):
        xc = x[:, c * LANES:(c + 1) * LANES]
        if exact:
            hi, lo = _split_bf16(xc)
            outs.append(_dot(hi, ones) + _dot(lo, ones))
        else:
            outs.append(_dot(xc.astype(BF16), ones))
    return outs[0] if len(outs) == 1 else jnp.concatenate(outs, axis=1)


FFN_TM = 512
FFN_FK = 256


def _ffn_kernel(x_ref, g_ref, wg_ref, wu_ref, wd_ref, o_ref):
    x = x_ref[...]
    h = _rms(x, g_ref[...]).astype(BF16)
    d_ff = wg_ref.shape[1]
    acc = jnp.zeros(x.shape, F32)
    for c in range(d_ff // FFN_FK):
        sl = slice(c * FFN_FK, (c + 1) * FFN_FK)
        gate = _dot(h, wg_ref[:, sl])
        up = _dot(h, wu_ref[:, sl])
        act = (gate * jax.nn.sigmoid(gate) * up).astype(BF16)
        acc = acc + _dot(act, wd_ref[sl, :])
    o_ref[...] = x + 0.5 * acc


def _ffn(x2, g, wg, wu, wd):
    m, d = x2.shape
    tm = min(FFN_TM, m)
    return pl.pallas_call(
        _ffn_kernel,
        out_shape=jax.ShapeDtypeStruct((m, d), F32),
        grid=(m // tm,),
        in_specs=[
            pl.BlockSpec((tm, d), lambda i: (i, 0)),
            _resident((1, d)),
            _resident(wg.shape),
            _resident(wu.shape),
            _resident(wd.shape),
        ],
        out_specs=pl.BlockSpec((tm, d), lambda i: (i, 0)),
        compiler_params=_params("parallel"),
        name="ffn",
    )(x2, g.reshape(1, d), wg, wu, wd)


PROJ_TM = 512
PROJ_NC = 256
SB_COLS = 3 * SB_HEADS * HEAD_DIM
DIL_COLS = DIL_HEADS * HEAD_DIM


def _proj_kernel(x_ref, g_ref, w_ref, qn_ref, kn_ref, o_ref):
    h = _rms(x_ref[...], g_ref[...]).astype(BF16)
    ones = _head_ones()
    scale = HEAD_DIM ** -0.5
    q_lo, k_lo, v_lo = SB_COLS, SB_COLS + DIL_COLS, SB_COLS + 2 * DIL_COLS
    for c in range(w_ref.shape[1] // PROJ_NC):
        lo = c * PROJ_NC
        sl = slice(lo, lo + PROJ_NC)
        p = _dot(h, w_ref[:, sl])
        if q_lo <= lo < v_lo:
            gain = qn_ref[...] if lo < k_lo else kn_ref[...]
            ms = _head_sum(p * p, ones, exact=False) * (1.0 / HEAD_DIM)
            p = p * lax.rsqrt(ms + NORM_EPS) * gain
        if lo < SB_HEADS * HEAD_DIM:
            p = p * (scale * LOG2E)
        elif q_lo <= lo < k_lo:
            p = p * scale
        o_ref[:, sl] = p.astype(BF16)


def _proj(x2, g, w_in, q_norm, k_norm):
    m, d = x2.shape
    n = w_in.shape[1]
    tm = min(PROJ_TM, m)
    tile = lambda v: jnp.tile(v.astype(F32), PROJ_NC // HEAD_DIM).reshape(1, PROJ_NC)
    return pl.pallas_call(
        _proj_kernel,
        out_shape=jax.ShapeDtypeStruct((m, n), BF16),
        grid=(m // tm,),
        in_specs=[
            pl.BlockSpec((tm, d), lambda i: (i, 0)),
            _resident((1, d)),
            _resident(w_in.shape),
            _resident((1, PROJ_NC)),
            _resident((1, PROJ_NC)),
        ],
        out_specs=pl.BlockSpec((tm, n), lambda i: (i, 0)),
        compiler_params=_params("parallel"),
        name="attn_proj",
    )(x2, g.reshape(1, d), w_in, tile(q_norm), tile(k_norm))


SB_TQ = 512


def _sb_kernel(q_ref, k_ref, v_ref, o_ref, qh_ref, acc_ref, run_ref):
    qi = pl.program_id(2)
    tq = q_ref.shape[0]
    tk = ATT_BLOCK
    nsub = tq // tk
    lane = lax.broadcasted_iota(jnp.int32, (1, LANES), 1)
    head0 = lane < HEAD_DIM
    q = q_ref[...]
    zero = jnp.zeros_like(q)
    qh_ref[0] = jnp.where(head0, q, zero)
    qh_ref[1] = jnp.where(head0, zero, q)
    acc_ref[...] = jnp.zeros_like(acc_ref)
    run_ref[...] = jnp.zeros_like(run_ref)
    r2 = lax.broadcasted_iota(jnp.int32, (2 * tk, 2 * tk), 0)
    c2 = lax.broadcasted_iota(jnp.int32, (2 * tk, 2 * tk), 1)
    sums = ((c2 >= tk) | ((r2 % tk) > c2)).astype(BF16)

    def step(j, r0, masked):
        start = pl.multiple_of(j * tk, tk)
        kb = k_ref[pl.ds(start, tk), :]
        vb = v_ref[pl.ds(start, tk), :]
        if masked:
            shape = (tq - r0, tk)
            strict = (lax.broadcasted_iota(jnp.int32, shape, 1)
                      < lax.broadcasted_iota(jnp.int32, shape, 0))
        zs = [_dot_nt(qh_ref[h, r0:, :], kb) for h in range(2)]
        log_betas = [jnp.minimum(z, 0.0) - jnp.log(1.0 + jnp.exp2(-jnp.abs(z))) * LOG2E for z in zs]
        log_keeps = [lb - z for lb, z in zip(log_betas, zs)]
        if masked:
            log_keeps = [jnp.where(strict, lk, 0.0) for lk in log_keeps]
        ats = [_dot(jnp.concatenate(_split_bf16(lk), axis=1), sums) for lk in log_keeps]
        for h in range(2):
            run = run_ref[h, r0:, :]
            w = jnp.exp2(log_betas[h] + ats[h][:, :tk] + run)
            if masked:
                w = jnp.where(strict, w, 0.0)
            acc_ref[h, r0:, :] += _dot(w.astype(BF16), vb)
            run_ref[h, r0:, :] = run + ats[h][:, tk:]

    for c in reversed(range(nsub)):
        step(qi * nsub + c, c * tk, True)

    def body(i, carry):
        step(qi * nsub - 1 - i, 0, False)
        return carry

    lax.fori_loop(0, qi * nsub, body, 0)
    o_ref[...] = jnp.where(head0, acc_ref[0], acc_ref[1]).astype(o_ref.dtype)


def _sb_attention(qkv):
    b, s, _ = qkv.shape
    pairs = SB_HEADS * HEAD_DIM // LANES
    tq = min(SB_TQ, s)
    return pl.pallas_call(
        _sb_kernel,
        out_shape=jax.ShapeDtypeStruct((b, s, SB_HEADS * HEAD_DIM), BF16),
        grid=(b, pairs, s // tq),
        in_specs=[
            pl.BlockSpec((None, tq, LANES), lambda bi, p, i: (bi, i, p)),
            pl.BlockSpec((None, s, LANES), lambda bi, p, i: (bi, 0, pairs + p)),
            pl.BlockSpec((None, s, LANES), lambda bi, p, i: (bi, 0, 2 * pairs + p)),
        ],
        out_specs=pl.BlockSpec((None, tq, LANES), lambda bi, p, i: (bi, i, p)),
        scratch_shapes=[pltpu.VMEM((2, tq, LANES), BF16), pltpu.VMEM((2, tq, LANES), F32),
                        pltpu.VMEM((2, tq, LANES), F32)],
        compiler_params=_params("parallel", "parallel", "arbitrary"),
        name="sb_attn",
    )(qkv, qkv, qkv)


def _t5_bucket(dist):
    max_exact = N_BUCKETS // 2
    d = jnp.maximum(dist, 1).astype(F32)
    large = max_exact + (jnp.log(d / max_exact) / math.log(MAX_DISTANCE / max_exact)
                         * (N_BUCKETS - max_exact)).astype(jnp.int32)
    large = jnp.minimum(large, N_BUCKETS - 1)
    return jnp.where(dist < max_exact, dist, large)


def _bias_blocks(rel_bias_group, dilation):
    qi = jnp.arange(ATT_BLOCK)[:, None]
    kj = jnp.arange(2 * ATT_BLOCK)[None, :] - ATT_BLOCK
    dist = qi - kj
    bucket = _t5_bucket(jnp.maximum(dist, 0) * dilation)
    onehot = (bucket[None] == jnp.arange(N_BUCKETS)[:, None, None]).astype(F32)
    bias = jnp.einsum('nh,nqk->hqk', rel_bias_group.astype(F32), onehot,
                      precision=lax.Precision.HIGHEST)
    return bias.reshape(2, 2, ATT_BLOCK, 2 * ATT_BLOCK)


def _dil_kernel(q_ref, kp_ref, kc_ref, vp_ref, vc_ref, bias_ref, o_ref, lse_ref):
    n = pl.program_id(3)
    blk = ATT_BLOCK
    q = q_ref[...]
    lane = lax.broadcasted_iota(jnp.int32, (1, LANES), 1)
    head0 = lane < HEAD_DIM
    zero = jnp.zeros_like(q)
    qh = (jnp.where(head0, q, zero), jnp.where(head0, zero, q))
    row = lax.broadcasted_iota(jnp.int32, (blk, 2 * blk), 0)
    col = lax.broadcasted_iota(jnp.int32, (blk, 2 * blk), 1)
    dist = row - col + blk
    mask = (dist >= 0) & (dist <= blk) & ((n > 0) | (col >= blk))
    kp, kc, vp, vc = kp_ref[...], kc_ref[...], vp_ref[...], vc_ref[...]
    outs, lses = [], []
    for h in range(2):
        z = jnp.concatenate([_dot_nt(qh[h], kp), _dot_nt(qh[h], kc)], axis=1)
        logits = jnp.where(mask, z + bias_ref[h], NEG_INF)
        m = jnp.max(logits, axis=1, keepdims=True)
        p = jnp.exp(logits - m)
        den = jnp.sum(p, axis=1, keepdims=True)
        pn = (p / den).astype(BF16)
        outs.append(_dot(pn[:, :blk], vp) + _dot(pn[:, blk:], vc))
        lses.append(m + jnp.log(den))
    o_ref[...] = jnp.where(head0, outs[0], outs[1])
    lse_ref[...] = jnp.where(head0, lses[0], lses[1])


def _dil_attention(q, k, v, col_blocks, bias):
    b, r, l, _ = q.shape
    pairs = DIL_GROUP_HEADS * HEAD_DIM // LANES
    blk = ATT_BLOCK
    qo, ko, vo = col_blocks
    cur = lambda off: (lambda bi, c, p, n: (bi, c, n, off + p))
    prev = lambda off: (lambda bi, c, p, n: (bi, c, jnp.maximum(n - 1, 0), off + p))
    tile = (None, None, blk, LANES)
    out = jax.ShapeDtypeStruct((b, r, l, pairs * LANES), F32)
    return pl.pallas_call(
        _dil_kernel,
        out_shape=(out, out),
        grid=(b, r, pairs, l // blk),
        in_specs=[
            pl.BlockSpec(tile, cur(qo)),
            pl.BlockSpec(tile, prev(ko)),
            pl.BlockSpec(tile, cur(ko)),
            pl.BlockSpec(tile, prev(vo)),
            pl.BlockSpec(tile, cur(vo)),
            pl.BlockSpec((None, 2, blk, 2 * blk), lambda bi, c, p, n: (p, 0, 0, 0)),
        ],
        out_specs=(pl.BlockSpec(tile, cur(0)), pl.BlockSpec(tile, cur(0))),
        compiler_params=_params("parallel", "parallel", "parallel", "arbitrary"),
        name="dil_attn",
    )(q, k, k, v, v, bias)


OUT_TM = 512


def _attn_out_kernel(x_ref, sb_ref, o0_ref, o1_ref, o2_ref, l0_ref, l1_ref, l2_ref, w_ref, y_ref):
    l0, l1, l2 = l0_ref[...], l1_ref[...], l2_ref[...]
    mx = jnp.maximum(jnp.maximum(l0, l1), l2)
    e0, e1, e2 = jnp.exp(l0 - mx), jnp.exp(l1 - mx), jnp.exp(l2 - mx)
    den = e0 + e1 + e2
    out_b = (e0 / den) * o0_ref[...] + (e1 / den) * o1_ref[...] + (e2 / den) * o2_ref[...]
    na = sb_ref.shape[1]
    y = _dot(sb_ref[...], w_ref[:na, :]) + _dot(out_b.astype(BF16), w_ref[na:, :])
    y_ref[...] = x_ref[...] + y


def _attn_out(x2, sb, outs, lses, w_out):
    m, d = x2.shape
    tm = min(OUT_TM, m)
    wide = sb.shape[1]
    row = lambda w: pl.BlockSpec((tm, w), lambda i: (i, 0))
    return pl.pallas_call(
        _attn_out_kernel,
        out_shape=jax.ShapeDtypeStruct((m, d), F32),
        grid=(m // tm,),
        in_specs=[row(d), row(wide)] + [row(wide)] * 6 + [_resident(w_out.shape)],
        out_specs=row(d),
        compiler_params=_params("parallel"),
        name="attn_out",
    )(x2, sb, *outs, *lses, w_out)


RW_TS = 256


def _rwkv_in_kernel(x_ref, xp_ref, g_ref, mix_ref, vec_ref, wr_ref, wk_ref, wv_ref,
                    w1_ref, w2_ref, a1_ref, a2_ref, g1_ref, g2_ref,
                    r_ref, ld_ref, k_ref, v_ref, na_ref, bb_ref, gate_ref, bonus_ref):
    si = pl.program_id(1)
    gain = g_ref[...]
    h = _rms(x_ref[...], gain)
    prev_last = _rms(xp_ref[...], gain)[SUBLANES - 1:, :] * (si > 0).astype(F32)
    rows = lax.broadcasted_iota(jnp.int32, h.shape, 0)
    shifted = jnp.where(rows == 0, prev_last, pltpu.roll(h, 1, axis=0))
    xx = shifted - h
    mixed = lambda i: (h + xx * mix_ref[i:i + 1, :]).astype(BF16)
    w0, a0, k_k, k_a, r_k = (vec_ref[i:i + 1, :] for i in range(5))

    r = _dot(mixed(0), wr_ref[...])
    k = _dot(mixed(2), wk_ref[...])
    v = _dot(mixed(3), wv_ref[...])
    lora_w = _dot(jnp.tanh(_dot(mixed(1), w1_ref[...])).astype(BF16), w2_ref[...])
    lora_a = _dot(_dot(mixed(4), a1_ref[...]).astype(BF16), a2_ref[...])
    gate = _dot(jax.nn.sigmoid(_dot(mixed(5), g1_ref[...])).astype(BF16), g2_ref[...])

    t = -(w0 + lora_w)
    w_log = -(jnp.maximum(t, 0.0) + jnp.log(1.0 + jnp.exp(-jnp.abs(t)))) - 0.5
    a = jax.nn.sigmoid(a0 + lora_a)
    ones = _head_ones()
    kk = k * k_k
    norm = jnp.sqrt(_head_sum(kk * kk, ones, exact=False))
    kk = kk / jnp.maximum(norm, 1e-12)
    k = k * (1.0 + (a - 1.0) * k_a)

    r_ref[...] = r
    ld_ref[...] = -jnp.exp(w_log)
    k_ref[...] = k
    v_ref[...] = v
    na_ref[...] = -kk
    bb_ref[...] = kk * a
    gate_ref[...] = gate
    bonus_ref[...] = _head_sum(r * k * r_k, ones, exact=True) * v


def _rwkv_in(x3, g, mix, vecs, wr, wk, wv, w1, w2, a1, a2, g1, g2):
    b, s, d = x3.shape
    ts = min(RW_TS, s)
    tile = pl.BlockSpec((None, ts, d), lambda bi, si: (bi, si, 0))
    prev = pl.BlockSpec((None, SUBLANES, d),
                        lambda bi, si: (bi, jnp.maximum(si * (ts // SUBLANES) - 1, 0), 0))
    out = jax.ShapeDtypeStruct((b, s, d), F32)
    weights = (wr, wk, wv, w1, w2, a1, a2, g1, g2)
    return pl.pallas_call(
        _rwkv_in_kernel,
        out_shape=(out,) * 8,
        grid=(b, s // ts),
        in_specs=[tile, prev, _resident((1, d)), _resident(mix.shape), _resident(vecs.shape)]
        + [_resident(w.shape) for w in weights],
        out_specs=(tile,) * 8,
        compiler_params=_params("parallel", "arbitrary"),
        name="rwkv_in",
    )(x3, x3, g.reshape(1, d), mix, vecs, *weights)


SCAN_ROWS = 512
SCAN_GROUP = 8


def _scan_kernel(r_ref, ld_ref, k_ref, v_ref, na_ref, bb_ref, y_ref, state_ref):
    c_len = SCAN_CHUNK
    two = 2 * c_len

    @pl.when(pl.program_id(2) == 0)
    def _():
        state_ref[...] = jnp.zeros_like(state_ref)

    lane = lax.broadcasted_iota(jnp.int32, (1, LANES), 1)
    head0 = lane < HEAD_DIM
    ri = lax.broadcasted_iota(jnp.int32, (two, two), 0)
    ci = lax.broadcasted_iota(jnp.int32, (two, two), 1)
    same_head = (ri // c_len) == (ci // c_len)
    strict = same_head & ((ri % c_len) > (ci % c_len))
    incl = same_head & ((ri % c_len) >= (ci % c_len))
    eye = ri == ci
    tri = (lax.broadcasted_iota(jnp.int32, (c_len, c_len), 0)
           >= lax.broadcasted_iota(jnp.int32, (c_len, c_len), 1)).astype(BF16)

    tri2 = jnp.concatenate([tri, tri], axis=1)

    def stack(x):
        return jnp.concatenate([jnp.where(head0, x, 0.0), jnp.where(head0, 0.0, x)], axis=0)

    def dup(x):
        return jnp.concatenate([x, x], axis=0)

    bf = lambda x: x.astype(BF16)
    each = lambda f, *ls: [f(*xs) for xs in zip(*ls)]

    def prepare(rw, ld, cum):
        r, k, v = r_ref[rw, :], k_ref[rw, :], v_ref[rw, :]
        na, bb = na_ref[rw, :], bb_ref[rw, :]
        total = cum[c_len - 1:, :]
        e_neg = jnp.exp(-cum)
        e_tail = jnp.exp(total - cum)
        at2 = bf(stack(na * jnp.exp(cum - ld)))
        rt2 = stack(r * jnp.exp(cum))
        tails = jnp.concatenate([stack(bb * e_tail), stack(k * e_tail)], axis=0)
        return dict(
            at2=at2, rt2=rt2, v2=bf(stack(v)),
            lhs=jnp.concatenate([at2, bf(rt2)], axis=0),
            rhs=jnp.concatenate([bf(dup(bb * e_neg)), bf(dup(k * e_neg))], axis=0),
            tails_t=bf(jnp.transpose(tails)),
            decay=jnp.where(eye, jnp.exp(total), 0.0))

    def transitions(rows):
        n = len(rows)
        lds = [ld_ref[rw, :] for rw in rows]
        parts = each(_split_bf16, lds)
        his = jnp.concatenate([p[0] for p in parts], axis=1) if n > 1 else parts[0][0]
        los = jnp.concatenate([p[1] for p in parts], axis=1) if n > 1 else parts[0][1]
        cum_all = _dot(tri2, jnp.concatenate([his, los], axis=0))
        cums = [cum_all[:, g * LANES:(g + 1) * LANES] for g in range(n)]
        ps = each(prepare, rows, lds, cums)

        grams = [_dot_nt(p["lhs"], p["rhs"]) for p in ps]
        n_abs = [jnp.where(strict, g[:two, :two], 0.0) for g in grams]
        t1s = [_dot(bf(jnp.where(strict, g[:two, two:], 0.0)), p["v2"]) for g, p in zip(grams, ps)]
        b_rbs = [bf(jnp.where(incl, g[two:, :two], 0.0)) for g in grams]
        b_rks = [bf(jnp.where(incl, g[two:, two:], 0.0)) for g in grams]

        pws = [_dot(bf(x), bf(x)) for x in n_abs]
        invs = [jnp.where(eye, 1.0, x) for x in n_abs]
        terms = 2
        while terms < c_len:
            last = 2 * terms >= c_len
            nxt_p, nxt_t = [], []
            for pw, inv in zip(pws, invs):
                pwb = bf(pw)
                if last:
                    nxt_t.append(inv + _dot(pwb, bf(inv)))
                else:
                    both = _dot(pwb, jnp.concatenate([pwb, bf(inv)], axis=1))
                    nxt_p.append(both[:, :two])
                    nxt_t.append(inv + both[:, two:])
            pws, invs = nxt_p, nxt_t
            terms *= 2

        wus = [bf(_dot(bf(inv), jnp.concatenate([p["at2"], bf(t1)], axis=1)))
               for inv, p, t1 in zip(invs, ps, t1s)]
        tops = [_dot(jnp.concatenate([p["tails_t"][:, :two], b_rb], axis=0), wu)
                for p, b_rb, wu in zip(ps, b_rbs, wus)]
        bots = [_dot(jnp.concatenate([p["tails_t"][:, two:], b_rk], axis=0), p["v2"])
                for p, b_rk in zip(ps, b_rks)]
        out = []
        for p, top, bot in zip(ps, tops, bots):
            m_mat = p["decay"] + top[:LANES, :LANES]
            n_mat = top[:LANES, LANES:] + bot[:LANES]
            q2 = p["rt2"] + top[LANES:, :LANES]
            y3 = top[LANES:, LANES:] + bot[LANES:]
            out.append((bf(jnp.concatenate([q2, m_mat], axis=0)), y3, n_mat))
        return out

    def group(gi, carry):
        base = gi * (SCAN_GROUP * c_len)
        rows = [pl.ds(pl.multiple_of(base + g * c_len, c_len), c_len) for g in range(SCAN_GROUP)]
        state = state_ref[...]
        for rw, (qm, y3, n_mat) in zip(rows, transitions(rows)):
            res = _dot(qm, bf(state))
            y2 = res[:two] + y3
            y_ref[rw, :] = y2[:c_len, :] + y2[c_len:, :]
            state = res[two:] + n_mat
        state_ref[...] = state
        return carry

    lax.fori_loop(0, r_ref.shape[0] // (SCAN_GROUP * c_len), group, 0)


def _scan(r, ld, k, v, na, bb):
    b, s, d = r.shape
    rows = min(SCAN_ROWS, s)
    assert rows % (SCAN_GROUP * SCAN_CHUNK) == 0 and s % rows == 0
    tile = pl.BlockSpec((None, rows, LANES), lambda bi, p, t: (bi, t, p))
    return pl.pallas_call(
        _scan_kernel,
        out_shape=jax.ShapeDtypeStruct((b, s, d), F32),
        grid=(b, d // LANES, s // rows),
        in_specs=[tile] * 6,
        out_specs=tile,
        scratch_shapes=[pltpu.VMEM((LANES, LANES), F32)],
        compiler_params=_params("parallel", "parallel", "arbitrary"),
        name="rwkv_scan",
    )(r, ld, k, v, na, bb)


RW_OUT_TM = 512


def _rwkv_out_kernel(x_ref, y_ref, bonus_ref, gate_ref, lg_ref, lb_ref, wo_ref, o_ref):
    y = y_ref[...]
    ones = _head_ones()
    mu = _head_sum(y, ones, exact=True) * (1.0 / HEAD_DIM)
    dlt = y - mu
    var = _head_sum(dlt * dlt, ones, exact=True) * (1.0 / HEAD_DIM)
    yn = dlt * lax.rsqrt(var + GN_EPS) * lg_ref[...] + lb_ref[...] + bonus_ref[...]
    o_ref[...] = x_ref[...] + _dot((yn * gate_ref[...]).astype(BF16), wo_ref[...])


def _rwkv_out(x2, y2, bonus2, gate2, lnx_g, lnx_b, wo):
    m, d = x2.shape
    tm = min(RW_OUT_TM, m)
    row = pl.BlockSpec((tm, d), lambda i: (i, 0))
    return pl.pallas_call(
        _rwkv_out_kernel,
        out_shape=jax.ShapeDtypeStruct((m, d), F32),
        grid=(m // tm,),
        in_specs=[row, row, row, row, _resident((1, d)), _resident((1, d)), _resident(wo.shape)],
        out_specs=row,
        compiler_params=_params("parallel"),
        name="rwkv_out",
    )(x2, y2, bonus2, gate2, lnx_g.reshape(1, d), lnx_b.reshape(1, d), wo)


def _attention_layer(x3, mix_g, rel_bias, w_in, q_norm, k_norm, w_out):
    b, s, d = x3.shape
    x2 = x3.reshape(b * s, d)
    qkv = _proj(x2, mix_g, w_in.astype(BF16), q_norm, k_norm).reshape(b, s, -1)
    sb = _sb_attention(qkv)
    outs, lses = [], []
    group_w = DIL_GROUP_HEADS * HEAD_DIM
    tiles = group_w // LANES
    for g, (_, r) in enumerate(DIL_PATTERNS):
        offs = tuple(SB_COLS + part * DIL_COLS + g * group_w for part in range(3))
        bias = _bias_blocks(rel_bias[:, g * DIL_GROUP_HEADS:(g + 1) * DIL_GROUP_HEADS], r)
        if r == 1:
            src = qkv.reshape(b, 1, s, -1)
            o, l = _dil_attention(src, src, src, tuple(o_ // LANES for o_ in offs), bias)
        else:
            sub = lambda lo: (qkv[:, :, lo:lo + group_w].reshape(b, s // r, r, group_w)
                              .transpose(0, 2, 1, 3))
            o, l = _dil_attention(sub(offs[0]), sub(offs[1]), sub(offs[2]), (0, 0, 0), bias)
        back = lambda t: t.transpose(0, 2, 1, 3).reshape(b * s, tiles * LANES)
        outs.append(back(o))
        lses.append(back(l))
    y = _attn_out(x2, sb.reshape(b * s, -1), outs, lses, w_out.astype(BF16))
    return y.reshape(b, s, d)


def _pad_cols(w, n):
    return jnp.pad(w, ((0, 0), (0, n - w.shape[1])))


def _pad_rows(w, n):
    return jnp.pad(w, ((0, n - w.shape[0]), (0, 0)))


def _rwkv_layer(x3, mix_g, mix, w0, w1, w2, a0, a1, a2, g1, g2, k_k, k_a, r_k,
                w_r, w_k, w_v, w_o, lnx_g, lnx_b):
    b, s, d = x3.shape
    bf = lambda w: w.astype(BF16)
    lora = lambda w: -(-w // LANES) * LANES
    vecs = jnp.stack([w0, a0, k_k, k_a, r_k.reshape(d), jnp.zeros_like(w0),
                      jnp.zeros_like(w0), jnp.zeros_like(w0)], axis=0).astype(F32)
    dw, da, dg = lora(w1.shape[1]), lora(a1.shape[1]), lora(g1.shape[1])
    r, ld, k, v, na, bb, gate, bonus = _rwkv_in(
        x3, mix_g, mix, vecs, bf(w_r), bf(w_k), bf(w_v),
        bf(_pad_cols(w1, dw)), bf(_pad_rows(w2, dw)),
        bf(_pad_cols(a1, da)), bf(_pad_rows(a2, da)),
        bf(_pad_cols(g1, dg)), bf(_pad_rows(g2, dg)))
    y = _scan(r, ld, k, v, na, bb)
    flat = lambda t: t.reshape(b * s, d)
    out = _rwkv_out(flat(x3), flat(y), flat(bonus), flat(gate), lnx_g, lnx_b, bf(w_o))
    return out.reshape(b, s, d)


def kernel(x, ffn_norm, ffn_w_gate, ffn_w_up, ffn_w_down, mix_norm, rel_bias, attn_w_in, attn_q_norm, attn_k_norm, attn_w_out, rw_mix, rw_w0, rw_w1, rw_w2, rw_a0, rw_a1, rw_a2, rw_g1, rw_g2, rw_kk, rw_ka, rw_rk, rw_wr, rw_wk, rw_wv, rw_wo, rw_lnx_g, rw_lnx_b):
    b, s, d = x.shape
    depth = ffn_norm.shape[0]

    def ffn(x3, layer, half):
        y = _ffn(x3.reshape(b * s, d), ffn_norm[layer, half],
                 ffn_w_gate[layer, half].astype(BF16), ffn_w_up[layer, half].astype(BF16),
                 ffn_w_down[layer, half].astype(BF16))
        return y.reshape(b, s, d)

    for layer in range(depth):
        x = ffn(x, layer, 0)
        if layer % 2 == 0:
            e = layer // 2
            x = _attention_layer(x, mix_norm[layer], rel_bias, attn_w_in[e], attn_q_norm[e],
                                 attn_k_norm[e], attn_w_out[e])
        else:
            o = layer // 2
            x = _rwkv_layer(x, mix_norm[layer], rw_mix[o], rw_w0[o], rw_w1[o], rw_w2[o],
                            rw_a0[o], rw_a1[o], rw_a2[o], rw_g1[o], rw_g2[o], rw_kk[o],
                            rw_ka[o], rw_rk[o], rw_wr[o], rw_wk[o], rw_wv[o], rw_wo[o],
                            rw_lnx_g[o], rw_lnx_b[o])
        x = ffn(x, layer, 1)
    return x
```

```python
import functools
import math

import jax
import jax.numpy as jnp
from jax import lax
from jax.experimental import pallas as pl
from jax.experimental.pallas import tpu as pltpu

F32 = jnp.float32
BF16 = jnp.bfloat16

HEAD_DIM = 64
LANES = 128
SUBLANES = 8
SB_HEADS = 4
DIL_HEADS = 12
DIL_PATTERNS = ((128, 1), (512, 4), (2048, 16))
DIL_GROUP_HEADS = 4
ATT_BLOCK = 128
N_BUCKETS = 32
MAX_DISTANCE = 2048
NORM_EPS = 1e-6
GN_EPS = 64e-5
NEG_INF = -1e30
LOG2E = math.log2(math.e)
SCAN_CHUNK = 64
VMEM_LIMIT = 56 * 1024 * 1024


def _params(*sem):
    return pltpu.CompilerParams(dimension_semantics=sem, vmem_limit_bytes=VMEM_LIMIT)


def _resident(shape):
    nd = len(shape)
    return pl.BlockSpec(shape, lambda *_: (0,) * nd, pipeline_mode=pl.Buffered(1))


def _rms(x, g):
    ms = jnp.mean(x * x, axis=-1, keepdims=True)
    return x * lax.rsqrt(ms + NORM_EPS) * g


def _dot(a, b):
    return jnp.dot(a, b, preferred_element_type=F32)


def _dot_nt(a, b):
    return lax.dot_general(a, b, (((1,), (1,)), ((), ())), preferred_element_type=F32)


def _split_bf16(x):
    hi = x.astype(BF16)
    lo = (x - hi.astype(F32)).astype(BF16)
    return hi, lo


def _head_ones():
    r = lax.broadcasted_iota(jnp.int32, (LANES, LANES), 0) // HEAD_DIM
    c = lax.broadcasted_iota(jnp.int32, (LANES, LANES), 1) // HEAD_DIM
    return (r == c).astype(BF16)


def _head_sum(x, ones, exact):
    outs = []
    for c in range(x.shape[1] // LANES):
        xc = x[:, c * LANES:(c + 1) * LANES]
        if exact:
            hi, lo = _split_bf16(xc)
            outs.append(_dot(hi, ones) + _dot(lo, ones))
        else:
            outs.append(_dot(xc.astype(BF16), ones))
    return outs[0] if len(outs) == 1 else jnp.concatenate(outs, axis=1)


FFN_TM = 512
FFN_FK = 256


def _ffn_kernel(x_ref, g_ref, wg_ref, wu_ref, wd_ref, o_ref):
    x = x_ref[...]
    h = _rms(x, g_ref[...]).astype(BF16)
    d_ff = wg_ref.shape[1]
    acc = jnp.zeros(x.shape, F32)
    for c in range(d_ff // FFN_FK):
        sl = slice(c * FFN_FK, (c + 1) * FFN_FK)
        gate = _dot(h, wg_ref[:, sl])
        up = _dot(h, wu_ref[:, sl])
        act = (gate * jax.nn.sigmoid(gate) * up).astype(BF16)
        acc = acc + _dot(act, wd_ref[sl, :])
    o_ref[...] = x + 0.5 * acc


def _ffn(x2, g, wg, wu, wd):
    m, d = x2.shape
    tm = min(FFN_TM, m)
    return pl.pallas_call(
        _ffn_kernel,
        out_shape=jax.ShapeDtypeStruct((m, d), F32),
        grid=(m // tm,),
        in_specs=[
            pl.BlockSpec((tm, d), lambda i: (i, 0)),
            _resident((1, d)),
            _resident(wg.shape),
            _resident(wu.shape),
            _resident(wd.shape),
        ],
        out_specs=pl.BlockSpec((tm, d), lambda i: (i, 0)),
        compiler_params=_params("parallel"),
        name="ffn",
    )(x2, g.reshape(1, d), wg, wu, wd)


PROJ_TM = 512
PROJ_NC = 256
SB_COLS = 3 * SB_HEADS * HEAD_DIM
DIL_COLS = DIL_HEADS * HEAD_DIM


def _proj_kernel(x_ref, g_ref, w_ref, qn_ref, kn_ref, o_ref):
    h = _rms(x_ref[...], g_ref[...]).astype(BF16)
    ones = _head_ones()
    scale = HEAD_DIM ** -0.5
    q_lo, k_lo, v_lo = SB_COLS, SB_COLS + DIL_COLS, SB_COLS + 2 * DIL_COLS
    n_chunks = w_ref.shape[1] // PROJ_NC
    cols = lambda c: slice(c * PROJ_NC, (c + 1) * PROJ_NC)

    def finish(c, p):
        lo = c * PROJ_NC
        if q_lo <= lo < v_lo:
            gain = qn_ref[...] if lo < k_lo else kn_ref[...]
            ms = _head_sum(p * p, ones, exact=False) * (1.0 / HEAD_DIM)
            p = p * lax.rsqrt(ms + NORM_EPS) * gain
        if lo < SB_HEADS * HEAD_DIM:
            p = p * (scale * LOG2E)
        elif q_lo <= lo < k_lo:
            p = p * scale
        o_ref[:, cols(c)] = p.astype(BF16)

    pending = _dot(h, w_ref[:, cols(0)])
    for c in range(1, n_chunks):
        nxt = _dot(h, w_ref[:, cols(c)])
        finish(c - 1, pending)
        pending = nxt
    finish(n_chunks - 1, pending)


def _proj(x2, g, w_in, q_norm, k_norm):
    m, d = x2.shape
    n = w_in.shape[1]
    tm = min(PROJ_TM, m)
    tile = lambda v: jnp.tile(v.astype(F32), PROJ_NC // HEAD_DIM).reshape(1, PROJ_NC)
    return pl.pallas_call(
        _proj_kernel,
        out_shape=jax.ShapeDtypeStruct((m, n), BF16),
        grid=(m // tm,),
        in_specs=[
            pl.BlockSpec((tm, d), lambda i: (i, 0)),
            _resident((1, d)),
            _resident(w_in.shape),
            _resident((1, PROJ_NC)),
            _resident((1, PROJ_NC)),
        ],
        out_specs=pl.BlockSpec((tm, n), lambda i: (i, 0)),
        compiler_params=_params("parallel"),
        name="attn_proj",
    )(x2, g.reshape(1, d), w_in, tile(q_norm), tile(k_norm))


SB_TQ = 512
SB_UNROLL = 2


def _sb_kernel(q_ref, k_ref, v_ref, o_ref, qh_ref, acc_ref, run_ref):
    qi = pl.program_id(2)
    tq = q_ref.shape[0]
    tk = ATT_BLOCK
    nsub = tq // tk
    lane = lax.broadcasted_iota(jnp.int32, (1, LANES), 1)
    head0 = lane < HEAD_DIM
    q = q_ref[...]
    zero = jnp.zeros_like(q)
    qh_ref[0] = jnp.where(head0, q, zero)
    qh_ref[1] = jnp.where(head0, zero, q)
    acc_ref[...] = jnp.zeros_like(acc_ref)
    run_ref[...] = jnp.zeros_like(run_ref)
    r2 = lax.broadcasted_iota(jnp.int32, (2 * tk, 2 * tk), 0)
    c2 = lax.broadcasted_iota(jnp.int32, (2 * tk, 2 * tk), 1)
    sums = ((c2 >= tk) | ((r2 % tk) > c2)).astype(BF16)

    def step(js, r0, masked):
        starts = [pl.multiple_of(j * tk, tk) for j in js]
        kbs = [k_ref[pl.ds(st, tk), :] for st in starts]
        vall = jnp.concatenate([v_ref[pl.ds(st, tk), :] for st in starts], axis=0)
        if masked:
            shape = (tq - r0, tk)
            strict = (lax.broadcasted_iota(jnp.int32, shape, 1)
                      < lax.broadcasted_iota(jnp.int32, shape, 0))
        chains = [(b, h) for b in range(len(js)) for h in range(2)]
        zs = [_dot_nt(qh_ref[h, r0:, :], kbs[b]) for b, h in chains]
        log_betas = [jnp.minimum(z, 0.0) - jnp.log(1.0 + jnp.exp2(-jnp.abs(z))) * LOG2E for z in zs]
        log_keeps = [lb - z for lb, z in zip(log_betas, zs)]
        if masked:
            log_keeps = [jnp.where(strict, lk, 0.0) for lk in log_keeps]
        ats = [_dot(jnp.concatenate(_split_bf16(lk), axis=1), sums) for lk in log_keeps]
        for h in range(2):
            run = run_ref[h, r0:, :]
            ws = []
            for b in range(len(js)):
                c = 2 * b + h
                w = jnp.exp2(log_betas[c] + ats[c][:, :tk] + run)
                if masked:
                    w = jnp.where(strict, w, 0.0)
                ws.append(w.astype(BF16))
                run = run + ats[c][:, tk:]
            acc_ref[h, r0:, :] += _dot(ws[0] if len(ws) == 1 else jnp.concatenate(ws, axis=1), vall)
            run_ref[h, r0:, :] = run

    for c in reversed(range(nsub)):
        step([qi * nsub + c], c * tk, True)

    def body(i, carry):
        j = qi * nsub - 1 - SB_UNROLL * i
        step([j - u for u in range(SB_UNROLL)], 0, False)
        return carry

    lax.fori_loop(0, qi * (nsub // SB_UNROLL), body, 0)
    o_ref[...] = jnp.where(head0, acc_ref[0], acc_ref[1]).astype(o_ref.dtype)


def _sb_attention(qkv):
    b, s, _ = qkv.shape
    pairs = SB_HEADS * HEAD_DIM // LANES
    tq = min(SB_TQ, s)
    return pl.pallas_call(
        _sb_kernel,
        out_shape=jax.ShapeDtypeStruct((b, s, SB_HEADS * HEAD_DIM), BF16),
        grid=(b, pairs, s // tq),
        in_specs=[
            pl.BlockSpec((None, tq, LANES), lambda bi, p, i: (bi, i, p)),
            pl.BlockSpec((None, s, LANES), lambda bi, p, i: (bi, 0, pairs + p)),
            pl.BlockSpec((None, s, LANES), lambda bi, p, i: (bi, 0, 2 * pairs + p)),
        ],
        out_specs=pl.BlockSpec((None, tq, LANES), lambda bi, p, i: (bi, i, p)),
        scratch_shapes=[pltpu.VMEM((2, tq, LANES), BF16), pltpu.VMEM((2, tq, LANES), F32),
                        pltpu.VMEM((2, tq, LANES), F32)],
        compiler_params=_params("parallel", "parallel", "arbitrary"),
        name="sb_attn",
    )(qkv, qkv, qkv)


def _t5_bucket(dist):
    max_exact = N_BUCKETS // 2
    d = jnp.maximum(dist, 1).astype(F32)
    large = max_exact + (jnp.log(d / max_exact) / math.log(MAX_DISTANCE / max_exact)
                         * (N_BUCKETS - max_exact)).astype(jnp.int32)
    large = jnp.minimum(large, N_BUCKETS - 1)
    return jnp.where(dist < max_exact, dist, large)


def _bias_blocks(rel_bias_group, dilation):
    qi = jnp.arange(ATT_BLOCK)[:, None]
    kj = jnp.arange(2 * ATT_BLOCK)[None, :] - ATT_BLOCK
    dist = qi - kj
    bucket = _t5_bucket(jnp.maximum(dist, 0) * dilation)
    onehot = (bucket[None] == jnp.arange(N_BUCKETS)[:, None, None]).astype(F32)
    bias = jnp.einsum('nh,nqk->hqk', rel_bias_group.astype(F32), onehot,
                      precision=lax.Precision.HIGHEST)
    return bias.reshape(2, 2, ATT_BLOCK, 2 * ATT_BLOCK)


DIL_ROWS = 512


def _dil_kernel(q_ref, kp_ref, kc_ref, vp_ref, vc_ref, bias_ref, o_ref, lse_ref):
    n = pl.program_id(3)
    blk = ATT_BLOCK
    nb = q_ref.shape[0] // blk
    lane = lax.broadcasted_iota(jnp.int32, (1, LANES), 1)
    head0 = lane < HEAD_DIM
    q = q_ref[...]
    zero = jnp.zeros_like(q)
    qh = (jnp.where(head0, q, zero), jnp.where(head0, zero, q))
    row = lax.broadcasted_iota(jnp.int32, (blk, 2 * blk), 0)
    col = lax.broadcasted_iota(jnp.int32, (blk, 2 * blk), 1)
    dist = row - col + blk
    window = (dist >= 0) & (dist <= blk)
    first = window & ((n > 0) | (col >= blk))
    keys = jnp.concatenate([kp_ref[...], kc_ref[...]], axis=0)
    vals = jnp.concatenate([vp_ref[...], vc_ref[...]], axis=0)
    chains = [(i, h) for i in range(nb) for h in range(2)]
    span = lambda i: slice(i * blk, (i + 2) * blk)
    zs = [_dot_nt(qh[h][i * blk:(i + 1) * blk], keys[span(i)]) for i, h in chains]
    logits = [jnp.where(first if i == 0 else window, z + bias_ref[h], NEG_INF)
              for (i, h), z in zip(chains, zs)]
    ms = [jnp.max(l, axis=1, keepdims=True) for l in logits]
    ps = [jnp.exp(l - m) for l, m in zip(logits, ms)]
    dens = [jnp.sum(p, axis=1, keepdims=True) for p in ps]
    outs = [_dot((p / den).astype(BF16), vals[span(i)]) for (i, _), p, den in zip(chains, ps, dens)]
    lses = [m + jnp.log(den) for m, den in zip(ms, dens)]
    for i in range(nb):
        rows = slice(i * blk, (i + 1) * blk)
        o_ref[rows, :] = jnp.where(head0, outs[2 * i], outs[2 * i + 1])
        lse_ref[rows, :] = jnp.where(head0, lses[2 * i], lses[2 * i + 1])


def _dil_attention(qkv, dilation, col_blocks, bias):
    b, s, w = qkv.shape
    r = dilation
    l = s // r
    pairs = DIL_GROUP_HEADS * HEAD_DIM // LANES
    blk = ATT_BLOCK
    rows = min(DIL_ROWS, l)
    nb = rows // blk
    wt = w // LANES
    qo, ko, vo = col_blocks
    cur = lambda off: (lambda bi, c, p, n: (bi, n, c * wt + off + p))
    prev = lambda off: (lambda bi, c, p, n: (bi, jnp.maximum(n * nb - 1, 0), c * wt + off + p))
    out_map = lambda bi, c, p, n: (bi, n, c * pairs + p)
    tile = (None, rows, LANES)
    one = (None, blk, LANES)
    view = qkv.reshape(b, l, r * w)
    out = jax.ShapeDtypeStruct((b, l, r * pairs * LANES), F32)
    o, lse = pl.pallas_call(
        _dil_kernel,
        out_shape=(out, out),
        grid=(b, r, pairs, l // rows),
        in_specs=[
            pl.BlockSpec(tile, cur(qo)),
            pl.BlockSpec(one, prev(ko)),
            pl.BlockSpec(tile, cur(ko)),
            pl.BlockSpec(one, prev(vo)),
            pl.BlockSpec(tile, cur(vo)),
            pl.BlockSpec((None, 2, blk, 2 * blk), lambda bi, c, p, n: (p, 0, 0, 0)),
        ],
        out_specs=(pl.BlockSpec(tile, out_map), pl.BlockSpec(tile, out_map)),
        compiler_params=_params("parallel", "parallel", "parallel", "arbitrary"),
        name="dil_attn",
    )(view, view, view, view, view, bias)
    return o.reshape(b * s, pairs * LANES), lse.reshape(b * s, pairs * LANES)


OUT_TM = 512


def _attn_out_kernel(x_ref, sb_ref, o0_ref, o1_ref, o2_ref, l0_ref, l1_ref, l2_ref, w_ref, y_ref):
    l0, l1, l2 = l0_ref[...], l1_ref[...], l2_ref[...]
    mx = jnp.maximum(jnp.maximum(l0, l1), l2)
    e0, e1, e2 = jnp.exp(l0 - mx), jnp.exp(l1 - mx), jnp.exp(l2 - mx)
    den = e0 + e1 + e2
    out_b = (e0 / den) * o0_ref[...] + (e1 / den) * o1_ref[...] + (e2 / den) * o2_ref[...]
    na = sb_ref.shape[1]
    y = _dot(sb_ref[...], w_ref[:na, :]) + _dot(out_b.astype(BF16), w_ref[na:, :])
    y_ref[...] = x_ref[...] + y


def _attn_out(x2, sb, outs, lses, w_out):
    m, d = x2.shape
    tm = min(OUT_TM, m)
    wide = sb.shape[1]
    row = lambda w: pl.BlockSpec((tm, w), lambda i: (i, 0))
    return pl.pallas_call(
        _attn_out_kernel,
        out_shape=jax.ShapeDtypeStruct((m, d), F32),
        grid=(m // tm,),
        in_specs=[row(d), row(wide)] + [row(wide)] * 6 + [_resident(w_out.shape)],
        out_specs=row(d),
        compiler_params=_params("parallel"),
        name="attn_out",
    )(x2, sb, *outs, *lses, w_out)


RW_TS = 256


def _rwkv_in_kernel(x_ref, xp_ref, g_ref, mix_ref, vec_ref, wr_ref, wk_ref, wv_ref,
                    w1_ref, w2_ref, a1_ref, a2_ref, g1_ref, g2_ref,
                    r_ref, ld_ref, k_ref, v_ref, na_ref, bb_ref, gate_ref, bonus_ref):
    si = pl.program_id(1)
    gain = g_ref[...]
    h = _rms(x_ref[...], gain)
    prev_last = _rms(xp_ref[...], gain)[SUBLANES - 1:, :] * (si > 0).astype(F32)
    rows = lax.broadcasted_iota(jnp.int32, h.shape, 0)
    shifted = jnp.where(rows == 0, prev_last, pltpu.roll(h, 1, axis=0))
    xx = shifted - h
    mixed = lambda i: (h + xx * mix_ref[i:i + 1, :]).astype(BF16)
    w0, a0, k_k, k_a, r_k = (vec_ref[i:i + 1, :] for i in range(5))

    r = _dot(mixed(0), wr_ref[...])
    k = _dot(mixed(2), wk_ref[...])
    v = _dot(mixed(3), wv_ref[...])
    lora_w = _dot(jnp.tanh(_dot(mixed(1), w1_ref[...])).astype(BF16), w2_ref[...])
    lora_a = _dot(_dot(mixed(4), a1_ref[...]).astype(BF16), a2_ref[...])
    gate = _dot(jax.nn.sigmoid(_dot(mixed(5), g1_ref[...])).astype(BF16), g2_ref[...])

    t = -(w0 + lora_w)
    w_log = -(jnp.maximum(t, 0.0) + jnp.log(1.0 + jnp.exp(-jnp.abs(t)))) - 0.5
    a = jax.nn.sigmoid(a0 + lora_a)
    ones = _head_ones()
    kk = k * k_k
    norm = jnp.sqrt(_head_sum(kk * kk, ones, exact=False))
    kk = kk / jnp.maximum(norm, 1e-12)
    k = k * (1.0 + (a - 1.0) * k_a)

    r_ref[...] = r.astype(r_ref.dtype)
    ld_ref[...] = -jnp.exp(w_log)
    k_ref[...] = k.astype(k_ref.dtype)
    v_ref[...] = v.astype(v_ref.dtype)
    na_ref[...] = (-kk).astype(na_ref.dtype)
    bb_ref[...] = (kk * a).astype(bb_ref.dtype)
    gate_ref[...] = gate.astype(gate_ref.dtype)
    bonus_ref[...] = _head_sum(r * k * r_k, ones, exact=True) * v


def _rwkv_in(x3, g, mix, vecs, wr, wk, wv, w1, w2, a1, a2, g1, g2):
    b, s, d = x3.shape
    ts = min(RW_TS, s)
    tile = pl.BlockSpec((None, ts, d), lambda bi, si: (bi, si, 0))
    prev = pl.BlockSpec((None, SUBLANES, d),
                        lambda bi, si: (bi, jnp.maximum(si * (ts // SUBLANES) - 1, 0), 0))
    out = lambda dt: jax.ShapeDtypeStruct((b, s, d), dt)
    weights = (wr, wk, wv, w1, w2, a1, a2, g1, g2)
    return pl.pallas_call(
        _rwkv_in_kernel,
        out_shape=(out(BF16), out(F32), out(BF16), out(BF16), out(BF16), out(BF16), out(BF16),
                   out(F32)),
        grid=(b, s // ts),
        in_specs=[tile, prev, _resident((1, d)), _resident(mix.shape), _resident(vecs.shape)]
        + [_resident(w.shape) for w in weights],
        out_specs=(tile,) * 8,
        compiler_params=_params("parallel", "arbitrary"),
        name="rwkv_in",
    )(x3, x3, g.reshape(1, d), mix, vecs, *weights)


SCAN_ROWS = 512


def _scan_kernel(tiles_per_seq, r_ref, ld_ref, k_ref, v_ref, na_ref, bb_ref, y_ref,
                 state_ref, qm_ref, y3_ref, n_ref):
    c_len = SCAN_CHUNK
    two = 2 * c_len
    n_chunks = r_ref.shape[0] // c_len
    step = pl.program_id(0)

    @pl.when(step == 0)
    def _():
        state_ref[...] = jnp.zeros_like(state_ref)
        qm_ref[...] = jnp.zeros_like(qm_ref)
        y3_ref[...] = jnp.zeros_like(y3_ref)
        n_ref[...] = jnp.zeros_like(n_ref)

    lane = lax.broadcasted_iota(jnp.int32, (1, LANES), 1)
    head0 = lane < HEAD_DIM
    ri = lax.broadcasted_iota(jnp.int32, (two, two), 0)
    ci = lax.broadcasted_iota(jnp.int32, (two, two), 1)
    same_head = (ri // c_len) == (ci // c_len)
    strict = same_head & ((ri % c_len) > (ci % c_len))
    incl = same_head & ((ri % c_len) >= (ci % c_len))
    eye = ri == ci
    tri = (lax.broadcasted_iota(jnp.int32, (c_len, c_len), 0)
           >= lax.broadcasted_iota(jnp.int32, (c_len, c_len), 1)).astype(BF16)
    tri2 = jnp.concatenate([tri, tri], axis=1)

    def stack(x):
        return jnp.concatenate([jnp.where(head0, x, 0.0), jnp.where(head0, 0.0, x)], axis=0)

    def dup(x):
        return jnp.concatenate([x, x], axis=0)

    bf = lambda x: x.astype(BF16)
    each = lambda f, *ls: [f(*xs) for xs in zip(*ls)]

    def prepare(rw, ld, cum):
        r, k, v, na, bb = (ref[rw, :].astype(F32) for ref in (r_ref, k_ref, v_ref, na_ref, bb_ref))
        total = cum[c_len - 1:, :]
        e_neg = jnp.exp(-cum)
        e_tail = jnp.exp(total - cum)
        at2 = bf(stack(na * jnp.exp(cum - ld)))
        rt2 = stack(r * jnp.exp(cum))
        tails = jnp.concatenate([stack(bb * e_tail), stack(k * e_tail)], axis=0)
        return dict(
            at2=at2, rt2=rt2, v2=bf(stack(v)),
            lhs=jnp.concatenate([at2, bf(rt2)], axis=0),
            rhs=jnp.concatenate([bf(dup(bb * e_neg)), bf(dup(k * e_neg))], axis=0),
            tails_t=bf(jnp.transpose(tails)),
            decay=jnp.where(eye, jnp.exp(total), 0.0))

    def transitions(rows, tick):
        n = len(rows)
        lds = [ld_ref[rw, :] for rw in rows]
        parts = each(_split_bf16, lds)
        his = jnp.concatenate([p[0] for p in parts], axis=1) if n > 1 else parts[0][0]
        los = jnp.concatenate([p[1] for p in parts], axis=1) if n > 1 else parts[0][1]
        cum_all = _dot(tri2, jnp.concatenate([his, los], axis=0))
        cums = [cum_all[:, g * LANES:(g + 1) * LANES] for g in range(n)]
        ps = each(prepare, rows, lds, cums)
        tick()

        grams = [_dot_nt(p["lhs"], p["rhs"]) for p in ps]
        n_abs = [jnp.where(strict, g[:two, :two], 0.0) for g in grams]
        t1s = [_dot(bf(jnp.where(strict, g[:two, two:], 0.0)), p["v2"]) for g, p in zip(grams, ps)]
        b_rbs = [bf(jnp.where(incl, g[two:, :two], 0.0)) for g in grams]
        b_rks = [bf(jnp.where(incl, g[two:, two:], 0.0)) for g in grams]
        tick()

        pws = [_dot(bf(x), bf(x)) for x in n_abs]
        invs = [jnp.where(eye, 1.0, x) for x in n_abs]
        tick()
        terms = 2
        while terms < c_len:
            last = 2 * terms >= c_len
            nxt_p, nxt_t = [], []
            for pw, inv in zip(pws, invs):
                pwb = bf(pw)
                if last:
                    nxt_t.append(inv + _dot(pwb, bf(inv)))
                else:
                    both = _dot(pwb, jnp.concatenate([pwb, bf(inv)], axis=1))
                    nxt_p.append(both[:, :two])
                    nxt_t.append(inv + both[:, two:])
            pws, invs = nxt_p, nxt_t
            terms *= 2
            tick()

        wus = [bf(_dot(bf(inv), jnp.concatenate([p["at2"], bf(t1)], axis=1)))
               for inv, p, t1 in zip(invs, ps, t1s)]
        tick()
        tops = [_dot(jnp.concatenate([p["tails_t"][:, :two], b_rb], axis=0), wu)
                for p, b_rb, wu in zip(ps, b_rbs, wus)]
        bots = [_dot(jnp.concatenate([p["tails_t"][:, two:], b_rk], axis=0), p["v2"])
                for p, b_rk in zip(ps, b_rks)]
        out = []
        for p, top, bot in zip(ps, tops, bots):
            m_mat = p["decay"] + top[:LANES, :LANES]
            n_mat = top[:LANES, LANES:] + bot[:LANES]
            q2 = p["rt2"] + top[LANES:, :LANES]
            y3 = top[LANES:, LANES:] + bot[LANES:]
            out.append((bf(jnp.concatenate([q2, m_mat], axis=0)), y3, n_mat))
        return out

    rows = [pl.ds(g * c_len, c_len) for g in range(n_chunks)]
    starts_seq = (step - 1) % tiles_per_seq == 0
    chain = {"state": jnp.where(starts_seq, 0.0, state_ref[...]), "next": 0}

    def chain_step():
        g = chain["next"]
        if g == n_chunks:
            return
        chain["next"] = g + 1
        res = _dot(qm_ref[g], bf(chain["state"]))
        y2 = res[:two] + y3_ref[g]
        y_ref[rows[g], :] = y2[:c_len, :] + y2[c_len:, :]
        chain["state"] = res[two:] + n_ref[g]

    trans = transitions(rows, chain_step)
    while chain["next"] < n_chunks:
        chain_step()
    state_ref[...] = chain["state"]
    for g, (qm, y3, n_mat) in enumerate(trans):
        qm_ref[g] = qm
        y3_ref[g] = y3
        n_ref[g] = n_mat


def _scan(r, ld, k, v, na, bb):
    b, s, d = r.shape
    rows = min(SCAN_ROWS, s)
    assert s % rows == 0 and rows % SCAN_CHUNK == 0
    pairs, tiles = d // LANES, s // rows
    n_tiles = b * pairs * tiles
    n_chunks = rows // SCAN_CHUNK

    def tile_of(j):
        return j // (pairs * tiles), j % tiles, (j // tiles) % pairs

    in_tile = pl.BlockSpec((None, rows, LANES), lambda i: tile_of(jnp.minimum(i, n_tiles - 1)))
    out_tile = pl.BlockSpec((None, rows, LANES), lambda i: tile_of(jnp.maximum(i - 1, 0)))
    return pl.pallas_call(
        functools.partial(_scan_kernel, tiles),
        out_shape=jax.ShapeDtypeStruct((b, s, d), F32),
        grid=(n_tiles + 1,),
        in_specs=[in_tile] * 6,
        out_specs=out_tile,
        scratch_shapes=[pltpu.VMEM((LANES, LANES), F32),
                        pltpu.VMEM((n_chunks, 2 * LANES, LANES), BF16),
                        pltpu.VMEM((n_chunks, LANES, LANES), F32),
                        pltpu.VMEM((n_chunks, LANES, LANES), F32)],
        compiler_params=_params("arbitrary"),
        name="rwkv_scan",
    )(r, ld, k, v, na, bb)


RW_OUT_TM = 512


def _rwkv_out_kernel(x_ref, y_ref, bonus_ref, gate_ref, lg_ref, lb_ref, wo_ref, o_ref):
    y = y_ref[...]
    ones = _head_ones()
    mu = _head_sum(y, ones, exact=True) * (1.0 / HEAD_DIM)
    dlt = y - mu
    var = _head_sum(dlt * dlt, ones, exact=True) * (1.0 / HEAD_DIM)
    yn = dlt * lax.rsqrt(var + GN_EPS) * lg_ref[...] + lb_ref[...] + bonus_ref[...]
    o_ref[...] = x_ref[...] + _dot((yn * gate_ref[...].astype(F32)).astype(BF16), wo_ref[...])


def _rwkv_out(x2, y2, bonus2, gate2, lnx_g, lnx_b, wo):
    m, d = x2.shape
    tm = min(RW_OUT_TM, m)
    row = pl.BlockSpec((tm, d), lambda i: (i, 0))
    return pl.pallas_call(
        _rwkv_out_kernel,
        out_shape=jax.ShapeDtypeStruct((m, d), F32),
        grid=(m // tm,),
        in_specs=[row, row, row, row, _resident((1, d)), _resident((1, d)), _resident(wo.shape)],
        out_specs=row,
        compiler_params=_params("parallel"),
        name="rwkv_out",
    )(x2, y2, bonus2, gate2, lnx_g.reshape(1, d), lnx_b.reshape(1, d), wo)


def _attention_layer(x3, mix_g, rel_bias, w_in, q_norm, k_norm, w_out):
    b, s, d = x3.shape
    x2 = x3.reshape(b * s, d)
    qkv = _proj(x2, mix_g, w_in.astype(BF16), q_norm, k_norm).reshape(b, s, -1)
    sb = _sb_attention(qkv)
    outs, lses = [], []
    group_w = DIL_GROUP_HEADS * HEAD_DIM
    for g, (_, r) in enumerate(DIL_PATTERNS):
        offs = tuple((SB_COLS + part * DIL_COLS + g * group_w) // LANES for part in range(3))
        bias = _bias_blocks(rel_bias[:, g * DIL_GROUP_HEADS:(g + 1) * DIL_GROUP_HEADS], r)
        o, l = _dil_attention(qkv, r, offs, bias)
        outs.append(o)
        lses.append(l)
    y = _attn_out(x2, sb.reshape(b * s, -1), outs, lses, w_out.astype(BF16))
    return y.reshape(b, s, d)


def _pad_cols(w, n):
    return jnp.pad(w, ((0, 0), (0, n - w.shape[1])))


def _pad_rows(w, n):
    return jnp.pad(w, ((0, n - w.shape[0]), (0, 0)))


def _rwkv_layer(x3, mix_g, mix, w0, w1, w2, a0, a1, a2, g1, g2, k_k, k_a, r_k,
                w_r, w_k, w_v, w_o, lnx_g, lnx_b):
    b, s, d = x3.shape
    bf = lambda w: w.astype(BF16)
    lora = lambda w: -(-w // LANES) * LANES
    vecs = jnp.stack([w0, a0, k_k, k_a, r_k.reshape(d), jnp.zeros_like(w0),
                      jnp.zeros_like(w0), jnp.zeros_like(w0)], axis=0).astype(F32)
    dw, da, dg = lora(w1.shape[1]), lora(a1.shape[1]), lora(g1.shape[1])
    r, ld, k, v, na, bb, gate, bonus = _rwkv_in(
        x3, mix_g, mix, vecs, bf(w_r), bf(w_k), bf(w_v),
        bf(_pad_cols(w1, dw)), bf(_pad_rows(w2, dw)),
        bf(_pad_cols(a1, da)), bf(_pad_rows(a2, da)),
        bf(_pad_cols(g1, dg)), bf(_pad_rows(g2, dg)))
    y = _scan(r, ld, k, v, na, bb)
    flat = lambda t: t.reshape(b * s, d)
    out = _rwkv_out(flat(x3), flat(y), flat(bonus), flat(gate), lnx_g, lnx_b, bf(w_o))
    return out.reshape(b, s, d)


def kernel(x, ffn_norm, ffn_w_gate, ffn_w_up, ffn_w_down, mix_norm, rel_bias, attn_w_in, attn_q_norm, attn_k_norm, attn_w_out, rw_mix, rw_w0, rw_w1, rw_w2, rw_a0, rw_a1, rw_a2, rw_g1, rw_g2, rw_kk, rw_ka, rw_rk, rw_wr, rw_wk, rw_wv, rw_wo, rw_lnx_g, rw_lnx_b):
    b, s, d = x.shape
    depth = ffn_norm.shape[0]

    def ffn(x3, layer, half):
        y = _ffn(x3.reshape(b * s, d), ffn_norm[layer, half],
                 ffn_w_gate[layer, half].astype(BF16), ffn_w_up[layer, half].astype(BF16),
                 ffn_w_down[layer, half].astype(BF16))
        return y.reshape(b, s, d)

    for layer in range(depth):
        x = ffn(x, layer, 0)
        if layer % 2 == 0:
            e = layer // 2
            x = _attention_layer(x, mix_norm[layer], rel_bias, attn_w_in[e], attn_q_norm[e],
                                 attn_k_norm[e], attn_w_out[e])
        else:
            o = layer // 2
            x = _rwkv_layer(x, mix_norm[layer], rw_mix[o], rw_w0[o], rw_w1[o], rw_w2[o],
                            rw_a0[o], rw_a1[o], rw_a2[o], rw_g1[o], rw_g2[o], rw_kk[o],
                            rw_ka[o], rw_rk[o], rw_wr[o], rw_wk[o], rw_wv[o], rw_wo[o],
                            rw_lnx_g[o], rw_lnx_b[o])
        x = ffn(x, layer, 1)
    return x
```

```python
import functools
import math

import jax
import jax.numpy as jnp
from jax import lax
from jax.experimental import pallas as pl
from jax.experimental.pallas import tpu as pltpu

F32 = jnp.float32
BF16 = jnp.bfloat16

HEAD_DIM = 64
LANES = 128
SUBLANES = 8
SB_HEADS = 4
DIL_HEADS = 12
DIL_PATTERNS = ((128, 1), (512, 4), (2048, 16))
DIL_GROUP_HEADS = 4
ATT_BLOCK = 128
N_BUCKETS = 32
MAX_DISTANCE = 2048
NORM_EPS = 1e-6
GN_EPS = 64e-5
NEG_INF = -1e30
LOG2E = math.log2(math.e)
SCAN_CHUNK = 64
VMEM_LIMIT = 56 * 1024 * 1024


def _params(*sem):
    return pltpu.CompilerParams(dimension_semantics=sem, vmem_limit_bytes=VMEM_LIMIT)


def _resident(shape):
    nd = len(shape)
    return pl.BlockSpec(shape, lambda *_: (0,) * nd, pipeline_mode=pl.Buffered(1))


def _rms(x, g):
    ms = jnp.mean(x * x, axis=-1, keepdims=True)
    return x * lax.rsqrt(ms + NORM_EPS) * g


def _dot(a, b):
    return jnp.dot(a, b, preferred_element_type=F32)


def _dot_nt(a, b):
    return lax.dot_general(a, b, (((1,), (1,)), ((), ())), preferred_element_type=F32)


def _split_bf16(x):
    hi = x.astype(BF16)
    lo = (x - hi.astype(F32)).astype(BF16)
    return hi, lo


def _head_ones():
    r = lax.broadcasted_iota(jnp.int32, (LANES, LANES), 0) // HEAD_DIM
    c = lax.broadcasted_iota(jnp.int32, (LANES, LANES), 1) // HEAD_DIM
    return (r == c).astype(BF16)


def _head_sum(x, ones, exact):
    outs = []
    for c in range(x.shape[1] // LANES):
        xc = x[:, c * LANES:(c + 1) * LANES]
        if exact:
            hi, lo = _split_bf16(xc)
            outs.append(_dot(hi, ones) + _dot(lo, ones))
        else:
            outs.append(_dot(xc.astype(BF16), ones))
    return outs[0] if len(outs) == 1 else jnp.concatenate(outs, axis=1)


FFN_TM = 512
FFN_FK = 256


def _ffn_kernel(x_ref, g_ref, wg_ref, wu_ref, wd_ref, o_ref):
    x = x_ref[...]
    h = _rms(x, g_ref[...]).astype(BF16)
    d_ff = wg_ref.shape[1]
    acc = jnp.zeros(x.shape, F32)
    for c in range(d_ff // FFN_FK):
        sl = slice(c * FFN_FK, (c + 1) * FFN_FK)
        gate = _dot(h, wg_ref[:, sl])
        up = _dot(h, wu_ref[:, sl])
        act = (gate * jax.nn.sigmoid(gate) * up).astype(BF16)
        acc = acc + _dot(act, wd_ref[sl, :])
    o_ref[...] = x + 0.5 * acc


def _ffn(x2, g, wg, wu, wd):
    m, d = x2.shape
    tm = min(FFN_TM, m)
    return pl.pallas_call(
        _ffn_kernel,
        out_shape=jax.ShapeDtypeStruct((m, d), F32),
        grid=(m // tm,),
        in_specs=[
            pl.BlockSpec((tm, d), lambda i: (i, 0)),
            _resident((1, d)),
            _resident(wg.shape),
            _resident(wu.shape),
            _resident(wd.shape),
        ],
        out_specs=pl.BlockSpec((tm, d), lambda i: (i, 0)),
        compiler_params=_params("parallel"),
        name="ffn",
    )(x2, g.reshape(1, d), wg, wu, wd)


PROJ_TM = 512
PROJ_NC = 256
SB_COLS = 3 * SB_HEADS * HEAD_DIM
DIL_COLS = DIL_HEADS * HEAD_DIM


GROUP_COLS = 3 * DIL_GROUP_HEADS * HEAD_DIM
DILATIONS = tuple(r for _, r in DIL_PATTERNS)
assert PROJ_NC == DIL_GROUP_HEADS * HEAD_DIM


def _proj_kernel(x_ref, g_ref, w_ref, qn_ref, kn_ref, nat_ref, *rest):
    stream_refs, pt_ref = rest[:-1], rest[-1]
    h = _rms(x_ref[...], g_ref[...]).astype(BF16)
    tm = x_ref.shape[0]
    ones = _head_ones()
    scale = HEAD_DIM ** -0.5
    streams = dict(zip([g for g, r in enumerate(DILATIONS) if r > 1], stream_refs))
    n_chunks = w_ref.shape[1] // PROJ_NC
    cols = lambda c: slice(c * PROJ_NC, (c + 1) * PROJ_NC)
    group_of = lambda c: c // 3 - 1

    def finish(c, p):
        part, grp = c % 3, group_of(c)
        if grp >= 0 and part < 2:
            norm_gain = qn_ref[...] if part == 0 else kn_ref[...]
            ms = _head_sum(p * p, ones, exact=False) * (1.0 / HEAD_DIM)
            p = p * lax.rsqrt(ms + NORM_EPS) * norm_gain
        if part == 0:
            p = p * (scale * LOG2E if grp < 0 else scale)
        if grp not in streams:
            nat_ref[:, cols(c)] = p.astype(BF16)
            return
        r = DILATIONS[grp]
        n = tm // r
        for j in range(PROJ_NC // LANES):
            buf = pt_ref.at[c % 2, j]
            buf[...] = p[:, j * LANES:(j + 1) * LANES]
            lanes = slice(part * PROJ_NC + j * LANES, part * PROJ_NC + (j + 1) * LANES)
            for cc in range(r):
                streams[grp][cc, :, lanes] = buf[pl.ds(cc, n, stride=r), :].astype(BF16)

    product = lambda c: _dot(h, w_ref[:, cols(c)])
    pending = product(0)
    for c in range(1, n_chunks):
        nxt = product(c)
        finish(c - 1, pending)
        pending = nxt
    finish(n_chunks - 1, pending)


def _proj(x3, g, w_in, q_norm, k_norm):
    b, s, d = x3.shape
    tm = min(PROJ_TM, s)
    tiles = s // tm
    assert all(DILATIONS[g] == 1 for g in range(len(DILATIONS)) if g < DILATIONS.count(1))
    group_w = DIL_GROUP_HEADS * HEAD_DIM
    pieces = [w_in[:, :SB_COLS]] + [
        w_in[:, SB_COLS + part * DIL_COLS + g * group_w:][:, :group_w]
        for g in range(len(DILATIONS)) for part in range(3)]
    w = jnp.concatenate(pieces, axis=1).astype(BF16)
    nat_w = SB_COLS + GROUP_COLS * DILATIONS.count(1)
    tile = lambda v: jnp.tile(v.astype(F32), PROJ_NC // HEAD_DIM).reshape(1, PROJ_NC)
    stream_rs = [r for r in DILATIONS if r > 1]
    outs = pl.pallas_call(
        _proj_kernel,
        out_shape=(jax.ShapeDtypeStruct((b, s, nat_w), BF16),)
        + tuple(jax.ShapeDtypeStruct((b, r, s // r, GROUP_COLS), BF16) for r in stream_rs),
        grid=(b, tiles),
        in_specs=[
            pl.BlockSpec((None, tm, d), lambda bi, t: (bi, t, 0)),
            _resident((1, d)),
            _resident(w.shape),
            _resident((1, PROJ_NC)),
            _resident((1, PROJ_NC)),
        ],
        out_specs=(pl.BlockSpec((None, tm, nat_w), lambda bi, t: (bi, t, 0)),)
        + tuple(pl.BlockSpec((None, r, tm // r, GROUP_COLS), lambda bi, t: (bi, 0, t, 0))
                for r in stream_rs),
        scratch_shapes=[pltpu.VMEM((2, PROJ_NC // LANES, tm, LANES), F32)],
        compiler_params=_params("parallel", "parallel"),
        name="attn_proj",
    )(x3, g.reshape(1, d), w, tile(q_norm), tile(k_norm))
    return outs[0], list(outs[1:])


SB_TQ = 512
SB_UNROLL = 4


def _sb_kernel(q_ref, k_ref, v_ref, o_ref, qh_ref, acc_ref, run_ref):
    qi = pl.program_id(2)
    tq = q_ref.shape[0]
    tk = ATT_BLOCK
    nsub = tq // tk
    lane = lax.broadcasted_iota(jnp.int32, (1, LANES), 1)
    head0 = lane < HEAD_DIM
    q = q_ref[...]
    zero = jnp.zeros_like(q)
    qh_ref[0] = jnp.where(head0, q, zero)
    qh_ref[1] = jnp.where(head0, zero, q)
    acc_ref[...] = jnp.zeros_like(acc_ref)
    run_ref[...] = jnp.zeros_like(run_ref)
    later = (lax.broadcasted_iota(jnp.int32, (tk, tk), 0)
             > lax.broadcasted_iota(jnp.int32, (tk, tk), 1)).astype(BF16)

    def step(blocks, masked):
        starts = [pl.multiple_of(j * tk, tk) for j, _ in blocks]
        kbs = [k_ref[pl.ds(st, tk), :] for st in starts]
        vbs = [v_ref[pl.ds(st, tk), :] for st in starts]
        chains = [(b, h) for b in range(len(blocks)) for h in range(2)]
        zs = [_dot_nt(qh_ref[h, blocks[b][1]:, :], kbs[b]) for b, h in chains]
        log_betas = [jnp.minimum(z, 0.0) - jnp.log(1.0 + jnp.exp2(-jnp.abs(z))) * LOG2E for z in zs]
        log_keeps = [lb - z for lb, z in zip(log_betas, zs)]
        if masked:
            stricts = [lax.broadcasted_iota(jnp.int32, z.shape, 1)
                       < lax.broadcasted_iota(jnp.int32, z.shape, 0) for z in zs]
            log_keeps = [jnp.where(s, lk, 0.0) for s, lk in zip(stricts, log_keeps)]
        afters = [_dot(lk.astype(BF16), later) for lk in log_keeps]
        totals = [jnp.sum(lk, axis=1, keepdims=True) for lk in log_keeps]
        same_rows = len({r0 for _, r0 in blocks}) == 1
        for h in range(2):
            ws = []
            for b, (_, r0) in enumerate(blocks):
                c = 2 * b + h
                run = run_ref[h, r0:, :]
                w = jnp.exp2(log_betas[c] + afters[c] + run)
                if masked:
                    w = jnp.where(stricts[c], w, 0.0)
                ws.append(w.astype(BF16))
                run_ref[h, r0:, :] = run + totals[c]
                if not same_rows:
                    acc_ref[h, r0:, :] += _dot(ws[-1], vbs[b])
            if same_rows:
                r0 = blocks[0][1]
                acc_ref[h, r0:, :] += _dot(jnp.concatenate(ws, axis=1), jnp.concatenate(vbs, axis=0))

    step([(qi * nsub + c, c * tk) for c in reversed(range(nsub))], True)

    def body(i, carry):
        j = qi * nsub - 1 - SB_UNROLL * i
        step([(j - u, 0) for u in range(SB_UNROLL)], False)
        return carry

    lax.fori_loop(0, qi * (nsub // SB_UNROLL), body, 0)
    o_ref[...] = jnp.where(head0, acc_ref[0], acc_ref[1]).astype(o_ref.dtype)


def _sb_attention(qkv):
    b, s, _ = qkv.shape
    pairs = SB_HEADS * HEAD_DIM // LANES
    tq = min(SB_TQ, s)
    return pl.pallas_call(
        _sb_kernel,
        out_shape=jax.ShapeDtypeStruct((b, s, SB_HEADS * HEAD_DIM), BF16),
        grid=(b, pairs, s // tq),
        in_specs=[
            pl.BlockSpec((None, tq, LANES), lambda bi, p, i: (bi, i, p)),
            pl.BlockSpec((None, s, LANES), lambda bi, p, i: (bi, 0, pairs + p)),
            pl.BlockSpec((None, s, LANES), lambda bi, p, i: (bi, 0, 2 * pairs + p)),
        ],
        out_specs=pl.BlockSpec((None, tq, LANES), lambda bi, p, i: (bi, i, p)),
        scratch_shapes=[pltpu.VMEM((2, tq, LANES), BF16), pltpu.VMEM((2, tq, LANES), F32),
                        pltpu.VMEM((2, tq, LANES), F32)],
        compiler_params=_params("parallel", "parallel", "arbitrary"),
        name="sb_attn",
    )(qkv, qkv, qkv)


def _t5_bucket(dist):
    max_exact = N_BUCKETS // 2
    d = jnp.maximum(dist, 1).astype(F32)
    large = max_exact + (jnp.log(d / max_exact) / math.log(MAX_DISTANCE / max_exact)
                         * (N_BUCKETS - max_exact)).astype(jnp.int32)
    large = jnp.minimum(large, N_BUCKETS - 1)
    return jnp.where(dist < max_exact, dist, large)


def _bias_blocks(rel_bias_group, dilation):
    qi = jnp.arange(ATT_BLOCK)[:, None]
    kj = jnp.arange(2 * ATT_BLOCK)[None, :] - ATT_BLOCK
    dist = qi - kj
    bucket = _t5_bucket(jnp.maximum(dist, 0) * dilation)
    onehot = (bucket[None] == jnp.arange(N_BUCKETS)[:, None, None]).astype(F32)
    bias = jnp.einsum('nh,nqk->hqk', rel_bias_group.astype(F32), onehot,
                      precision=lax.Precision.HIGHEST)
    return bias.reshape(2, 2, ATT_BLOCK, 2 * ATT_BLOCK)


DIL_ROWS = 512


def _dil_kernel(q_ref, kp_ref, kc_ref, vp_ref, vc_ref, bias_ref, o_ref, lse_ref):
    n = pl.program_id(3)
    blk = ATT_BLOCK
    nb = q_ref.shape[0] // blk
    lane = lax.broadcasted_iota(jnp.int32, (1, LANES), 1)
    head0 = lane < HEAD_DIM
    q = q_ref[...]
    zero = jnp.zeros_like(q)
    qh = (jnp.where(head0, q, zero), jnp.where(head0, zero, q))
    row = lax.broadcasted_iota(jnp.int32, (blk, 2 * blk), 0)
    col = lax.broadcasted_iota(jnp.int32, (blk, 2 * blk), 1)
    dist = row - col + blk
    window = (dist >= 0) & (dist <= blk)
    first = window & ((n > 0) | (col >= blk))
    keys = jnp.concatenate([kp_ref[...], kc_ref[...]], axis=0)
    vals = jnp.concatenate([vp_ref[...], vc_ref[...]], axis=0)
    chains = [(i, h) for i in range(nb) for h in range(2)]
    span = lambda i: slice(i * blk, (i + 2) * blk)
    zs = [_dot_nt(qh[h][i * blk:(i + 1) * blk], keys[span(i)]) for i, h in chains]
    logits = [jnp.where(first if i == 0 else window, z + bias_ref[h], NEG_INF)
              for (i, h), z in zip(chains, zs)]
    ms = [jnp.max(l, axis=1, keepdims=True) for l in logits]
    ps = [jnp.exp(l - m) for l, m in zip(logits, ms)]
    dens = [jnp.sum(p, axis=1, keepdims=True) for p in ps]
    outs = [_dot((p / den).astype(BF16), vals[span(i)]) for (i, _), p, den in zip(chains, ps, dens)]
    lses = [m + jnp.log(den) for m, den in zip(ms, dens)]
    for i in range(nb):
        rows = slice(i * blk, (i + 1) * blk)
        o_ref[rows, :] = jnp.where(head0, outs[2 * i], outs[2 * i + 1])
        lse_ref[rows, :] = jnp.where(head0, lses[2 * i], lses[2 * i + 1])


def _dil_attention(src, col_blocks, bias):
    b, r, l, _ = src.shape
    pairs = DIL_GROUP_HEADS * HEAD_DIM // LANES
    blk = ATT_BLOCK
    rows = min(DIL_ROWS, l)
    nb = rows // blk
    qo, ko, vo = col_blocks
    cur = lambda off: (lambda bi, c, p, n: (bi, c, n, off + p))
    prev = lambda off: (lambda bi, c, p, n: (bi, c, jnp.maximum(n * nb - 1, 0), off + p))
    tile = (None, None, rows, LANES)
    one = (None, None, blk, LANES)
    out = jax.ShapeDtypeStruct((b, r, l, pairs * LANES), F32)
    return pl.pallas_call(
        _dil_kernel,
        out_shape=(out, out),
        grid=(b, r, pairs, l // rows),
        in_specs=[
            pl.BlockSpec(tile, cur(qo)),
            pl.BlockSpec(one, prev(ko)),
            pl.BlockSpec(tile, cur(ko)),
            pl.BlockSpec(one, prev(vo)),
            pl.BlockSpec(tile, cur(vo)),
            pl.BlockSpec((None, 2, blk, 2 * blk), lambda bi, c, p, n: (p, 0, 0, 0)),
        ],
        out_specs=(pl.BlockSpec(tile, cur(0)), pl.BlockSpec(tile, cur(0))),
        compiler_params=_params("parallel", "parallel", "parallel", "arbitrary"),
        name="dil_attn",
    )(src, src, src, src, src, bias)


OUT_TM = 512


def _attn_out_kernel(x_ref, sb_ref, o0_ref, o1_ref, o2_ref, l0_ref, l1_ref, l2_ref, w_ref, y_ref,
                     *order_refs):
    tm = x_ref.shape[0]
    scratch = list(order_refs)

    def natural(ref):
        r = ref.shape[0]
        if r == 1:
            return ref[0]
        buf = scratch.pop()
        for c in range(r):
            for j in range(buf.shape[0]):
                buf[j, pl.ds(c, tm // r, stride=r), :] = ref[c, :, j * LANES:(j + 1) * LANES]
        return jnp.concatenate([buf[j] for j in range(buf.shape[0])], axis=1)

    l0, l1, l2 = natural(l0_ref), natural(l1_ref), natural(l2_ref)
    mx = jnp.maximum(jnp.maximum(l0, l1), l2)
    e0, e1, e2 = jnp.exp(l0 - mx), jnp.exp(l1 - mx), jnp.exp(l2 - mx)
    den = e0 + e1 + e2
    out_b = ((e0 / den) * natural(o0_ref) + (e1 / den) * natural(o1_ref)
             + (e2 / den) * natural(o2_ref))
    na = sb_ref.shape[1]
    y = _dot(sb_ref[...], w_ref[:na, :]) + _dot(out_b.astype(BF16), w_ref[na:, :])
    y_ref[...] = x_ref[...] + y


def _attn_out(x3, sb, outs, lses, w_out):
    b, s, d = x3.shape
    tm = min(OUT_TM, s)
    wide = sb.shape[2]
    row = lambda w: pl.BlockSpec((None, tm, w), lambda bi, t: (bi, t, 0))
    stream = lambda a: pl.BlockSpec((None, a.shape[1], tm // a.shape[1], wide),
                                    lambda bi, t: (bi, 0, t, 0))
    n_buffers = 2 * sum(1 for a in outs if a.shape[1] > 1)
    return pl.pallas_call(
        _attn_out_kernel,
        out_shape=jax.ShapeDtypeStruct((b, s, d), F32),
        grid=(b, s // tm),
        in_specs=[row(d), row(wide)] + [stream(a) for a in (*outs, *lses)]
        + [_resident(w_out.shape)],
        out_specs=row(d),
        scratch_shapes=[pltpu.VMEM((wide // LANES, tm, LANES), F32)] * n_buffers,
        compiler_params=_params("parallel", "parallel"),
        name="attn_out",
    )(x3, sb, *outs, *lses, w_out)


RW_TS = 512


def _rwkv_in_kernel(x_ref, xp_ref, g_ref, mix_ref, vec_ref, wr_ref, wk_ref, wv_ref,
                    w1_ref, w2_ref, a1_ref, a2_ref, g1_ref, g2_ref,
                    r_ref, ld_ref, k_ref, v_ref, na_ref, bb_ref, gate_ref, bonus_ref):
    si = pl.program_id(1)
    gain = g_ref[...]
    h = _rms(x_ref[...], gain)
    prev_last = _rms(xp_ref[...], gain)[SUBLANES - 1:, :] * (si > 0).astype(F32)
    rows = lax.broadcasted_iota(jnp.int32, h.shape, 0)
    shifted = jnp.where(rows == 0, prev_last, pltpu.roll(h, 1, axis=0))
    xx = shifted - h
    mixed = lambda i: (h + xx * mix_ref[i:i + 1, :]).astype(BF16)
    w0, a0, k_k, k_a, r_k = (vec_ref[i:i + 1, :] for i in range(5))

    r = _dot(mixed(0), wr_ref[...])
    k = _dot(mixed(2), wk_ref[...])
    v = _dot(mixed(3), wv_ref[...])
    lora_w = _dot(jnp.tanh(_dot(mixed(1), w1_ref[...])).astype(BF16), w2_ref[...])
    lora_a = _dot(_dot(mixed(4), a1_ref[...]).astype(BF16), a2_ref[...])
    gate = _dot(jax.nn.sigmoid(_dot(mixed(5), g1_ref[...])).astype(BF16), g2_ref[...])

    t = -(w0 + lora_w)
    w_log = -(jnp.maximum(t, 0.0) + jnp.log(1.0 + jnp.exp(-jnp.abs(t)))) - 0.5
    a = jax.nn.sigmoid(a0 + lora_a)
    ones = _head_ones()
    kk = k * k_k
    norm = jnp.sqrt(_head_sum(kk * kk, ones, exact=False))
    kk = kk / jnp.maximum(norm, 1e-12)
    k = k * (1.0 + (a - 1.0) * k_a)

    r_ref[...] = r.astype(r_ref.dtype)
    ld_ref[...] = -jnp.exp(w_log)
    k_ref[...] = k.astype(k_ref.dtype)
    v_ref[...] = v.astype(v_ref.dtype)
    na_ref[...] = (-kk).astype(na_ref.dtype)
    bb_ref[...] = (kk * a).astype(bb_ref.dtype)
    gate_ref[...] = gate.astype(gate_ref.dtype)
    bonus_ref[...] = _head_sum(r * k * r_k, ones, exact=True) * v


def _rwkv_in(x3, g, mix, vecs, wr, wk, wv, w1, w2, a1, a2, g1, g2):
    b, s, d = x3.shape
    ts = min(RW_TS, s)
    tile = pl.BlockSpec((None, ts, d), lambda bi, si: (bi, si, 0))
    prev = pl.BlockSpec((None, SUBLANES, d),
                        lambda bi, si: (bi, jnp.maximum(si * (ts // SUBLANES) - 1, 0), 0))
    out = lambda dt: jax.ShapeDtypeStruct((b, s, d), dt)
    weights = (wr, wk, wv, w1, w2, a1, a2, g1, g2)
    return pl.pallas_call(
        _rwkv_in_kernel,
        out_shape=(out(BF16), out(F32), out(BF16), out(BF16), out(BF16), out(BF16), out(BF16),
                   out(F32)),
        grid=(b, s // ts),
        in_specs=[tile, prev, _resident((1, d)), _resident(mix.shape), _resident(vecs.shape)]
        + [_resident(w.shape) for w in weights],
        out_specs=(tile,) * 8,
        compiler_params=_params("parallel", "arbitrary"),
        name="rwkv_in",
    )(x3, x3, g.reshape(1, d), mix, vecs, *weights)


SCAN_ROWS = 512


def _scan_kernel(tiles_per_seq, r_ref, ld_ref, k_ref, v_ref, na_ref, bb_ref, y_ref,
                 state_ref, qm_ref, y3_ref, n_ref):
    c_len = SCAN_CHUNK
    two = 2 * c_len
    n_chunks = r_ref.shape[0] // c_len
    step = pl.program_id(0)

    @pl.when(step == 0)
    def _():
        state_ref[...] = jnp.zeros_like(state_ref)
        qm_ref[...] = jnp.zeros_like(qm_ref)
        y3_ref[...] = jnp.zeros_like(y3_ref)
        n_ref[...] = jnp.zeros_like(n_ref)

    lane = lax.broadcasted_iota(jnp.int32, (1, LANES), 1)
    head0 = lane < HEAD_DIM
    ri = lax.broadcasted_iota(jnp.int32, (two, two), 0)
    ci = lax.broadcasted_iota(jnp.int32, (two, two), 1)
    same_head = (ri // c_len) == (ci // c_len)
    strict = same_head & ((ri % c_len) > (ci % c_len))
    incl = same_head & ((ri % c_len) >= (ci % c_len))
    eye = ri == ci
    tri = (lax.broadcasted_iota(jnp.int32, (c_len, c_len), 0)
           >= lax.broadcasted_iota(jnp.int32, (c_len, c_len), 1)).astype(BF16)
    tri2 = jnp.concatenate([tri, tri], axis=1)

    def stack(x):
        return jnp.concatenate([jnp.where(head0, x, 0.0), jnp.where(head0, 0.0, x)], axis=0)

    def dup(x):
        return jnp.concatenate([x, x], axis=0)

    bf = lambda x: x.astype(BF16)
    each = lambda f, *ls: [f(*xs) for xs in zip(*ls)]

    def prepare(rw, ld, cum):
        r, k, v, na, bb = (ref[rw, :].astype(F32) for ref in (r_ref, k_ref, v_ref, na_ref, bb_ref))
        total = cum[c_len - 1:, :]
        e_neg = jnp.exp(-cum)
        e_tail = jnp.exp(total - cum)
        at2 = bf(stack(na * jnp.exp(cum - ld)))
        rt2 = stack(r * jnp.exp(cum))
        tails = jnp.concatenate([stack(bb * e_tail), stack(k * e_tail)], axis=0)
        return dict(
            at2=at2, rt2=rt2, v2=bf(stack(v)),
            lhs=jnp.concatenate([at2, bf(rt2)], axis=0),
            rhs=jnp.concatenate([bf(dup(bb * e_neg)), bf(dup(k * e_neg))], axis=0),
            tails_t=bf(jnp.transpose(tails)),
            decay=jnp.where(eye, jnp.exp(total), 0.0))

    def transitions(rows, tick):
        n = len(rows)
        lds = [ld_ref[rw, :] for rw in rows]
        parts = each(_split_bf16, lds)
        his = jnp.concatenate([p[0] for p in parts], axis=1) if n > 1 else parts[0][0]
        los = jnp.concatenate([p[1] for p in parts], axis=1) if n > 1 else parts[0][1]
        cum_all = _dot(tri2, jnp.concatenate([his, los], axis=0))
        cums = [cum_all[:, g * LANES:(g + 1) * LANES] for g in range(n)]
        ps = each(prepare, rows, lds, cums)
        tick()

        grams = [_dot_nt(p["lhs"], p["rhs"]) for p in ps]
        n_abs = [jnp.where(strict, g[:two, :two], 0.0) for g in grams]
        t1s = [_dot(bf(jnp.where(strict, g[:two, two:], 0.0)), p["v2"]) for g, p in zip(grams, ps)]
        b_rbs = [bf(jnp.where(incl, g[two:, :two], 0.0)) for g in grams]
        b_rks = [bf(jnp.where(incl, g[two:, two:], 0.0)) for g in grams]
        tick()

        pws = [_dot(bf(x), bf(x)) for x in n_abs]
        invs = [jnp.where(eye, 1.0, x) for x in n_abs]
        tick()
        terms = 2
        while terms < c_len:
            last = 2 * terms >= c_len
            nxt_p, nxt_t = [], []
            for pw, inv in zip(pws, invs):
                pwb = bf(pw)
                if last:
                    nxt_t.append(inv + _dot(pwb, bf(inv)))
                else:
                    both = _dot(pwb, jnp.concatenate([pwb, bf(inv)], axis=1))
                    nxt_p.append(both[:, :two])
                    nxt_t.append(inv + both[:, two:])
            pws, invs = nxt_p, nxt_t
            terms *= 2
            tick()

        wus = [bf(_dot(bf(inv), jnp.concatenate([p["at2"], bf(t1)], axis=1)))
               for inv, p, t1 in zip(invs, ps, t1s)]
        tick()
        tops = [_dot(jnp.concatenate([p["tails_t"][:, :two], b_rb], axis=0), wu)
                for p, b_rb, wu in zip(ps, b_rbs, wus)]
        bots = [_dot(jnp.concatenate([p["tails_t"][:, two:], b_rk], axis=0), p["v2"])
                for p, b_rk in zip(ps, b_rks)]
        out = []
        for p, top, bot in zip(ps, tops, bots):
            m_mat = p["decay"] + top[:LANES, :LANES]
            n_mat = top[:LANES, LANES:] + bot[:LANES]
            q2 = p["rt2"] + top[LANES:, :LANES]
            y3 = top[LANES:, LANES:] + bot[LANES:]
            out.append((bf(jnp.concatenate([q2, m_mat], axis=0)), y3, n_mat))
        return out

    rows = [pl.ds(g * c_len, c_len) for g in range(n_chunks)]
    starts_seq = (step - 1) % tiles_per_seq == 0
    chain = {"state": jnp.where(starts_seq, 0.0, state_ref[...]), "next": 0}

    def chain_step():
        g = chain["next"]
        if g == n_chunks:
            return
        chain["next"] = g + 1
        res = _dot(qm_ref[g], bf(chain["state"]))
        y2 = res[:two] + y3_ref[g]
        y_ref[rows[g], :] = y2[:c_len, :] + y2[c_len:, :]
        chain["state"] = res[two:] + n_ref[g]

    trans = transitions(rows, chain_step)
    while chain["next"] < n_chunks:
        chain_step()
    state_ref[...] = chain["state"]
    for g, (qm, y3, n_mat) in enumerate(trans):
        qm_ref[g] = qm
        y3_ref[g] = y3
        n_ref[g] = n_mat


def _scan(r, ld, k, v, na, bb):
    b, s, d = r.shape
    rows = min(SCAN_ROWS, s)
    assert s % rows == 0 and rows % SCAN_CHUNK == 0
    pairs, tiles = d // LANES, s // rows
    n_tiles = b * pairs * tiles
    n_chunks = rows // SCAN_CHUNK

    def tile_of(j):
        return j // (pairs * tiles), j % tiles, (j // tiles) % pairs

    in_tile = pl.BlockSpec((None, rows, LANES), lambda i: tile_of(jnp.minimum(i, n_tiles - 1)))
    out_tile = pl.BlockSpec((None, rows, LANES), lambda i: tile_of(jnp.maximum(i - 1, 0)))
    return pl.pallas_call(
        functools.partial(_scan_kernel, tiles),
        out_shape=jax.ShapeDtypeStruct((b, s, d), F32),
        grid=(n_tiles + 1,),
        in_specs=[in_tile] * 6,
        out_specs=out_tile,
        scratch_shapes=[pltpu.VMEM((LANES, LANES), F32),
                        pltpu.VMEM((n_chunks, 2 * LANES, LANES), BF16),
                        pltpu.VMEM((n_chunks, LANES, LANES), F32),
                        pltpu.VMEM((n_chunks, LANES, LANES), F32)],
        compiler_params=_params("arbitrary"),
        name="rwkv_scan",
    )(r, ld, k, v, na, bb)


RW_OUT_TM = 512


def _rwkv_out_kernel(x_ref, y_ref, bonus_ref, gate_ref, lg_ref, lb_ref, wo_ref, o_ref):
    y = y_ref[...]
    ones = _head_ones()
    mu = _head_sum(y, ones, exact=True) * (1.0 / HEAD_DIM)
    dlt = y - mu
    var = _head_sum(dlt * dlt, ones, exact=True) * (1.0 / HEAD_DIM)
    yn = dlt * lax.rsqrt(var + GN_EPS) * lg_ref[...] + lb_ref[...] + bonus_ref[...]
    o_ref[...] = x_ref[...] + _dot((yn * gate_ref[...].astype(F32)).astype(BF16), wo_ref[...])


def _rwkv_out(x2, y2, bonus2, gate2, lnx_g, lnx_b, wo):
    m, d = x2.shape
    tm = min(RW_OUT_TM, m)
    row = pl.BlockSpec((tm, d), lambda i: (i, 0))
    return pl.pallas_call(
        _rwkv_out_kernel,
        out_shape=jax.ShapeDtypeStruct((m, d), F32),
        grid=(m // tm,),
        in_specs=[row, row, row, row, _resident((1, d)), _resident((1, d)), _resident(wo.shape)],
        out_specs=row,
        compiler_params=_params("parallel"),
        name="rwkv_out",
    )(x2, y2, bonus2, gate2, lnx_g.reshape(1, d), lnx_b.reshape(1, d), wo)


def _attention_layer(x3, mix_g, rel_bias, w_in, q_norm, k_norm, w_out):
    b, s, d = x3.shape
    nat, streams = _proj(x3, mix_g, w_in, q_norm, k_norm)
    sb = _sb_attention(nat)
    outs, lses = [], []
    group_w = DIL_GROUP_HEADS * HEAD_DIM
    for g, r in enumerate(DILATIONS):
        if r == 1:
            src = nat.reshape(b, 1, s, -1)
            base = SB_COLS + GROUP_COLS * g
        else:
            src, base = streams.pop(0), 0
        offs = tuple((base + part * group_w) // LANES for part in range(3))
        bias = _bias_blocks(rel_bias[:, g * DIL_GROUP_HEADS:(g + 1) * DIL_GROUP_HEADS], r)
        o, l = _dil_attention(src, offs, bias)
        outs.append(o)
        lses.append(l)
    return _attn_out(x3, sb, outs, lses, w_out.astype(BF16))


def _pad_cols(w, n):
    return jnp.pad(w, ((0, 0), (0, n - w.shape[1])))


def _pad_rows(w, n):
    return jnp.pad(w, ((0, n - w.shape[0]), (0, 0)))


def _rwkv_layer(x3, mix_g, mix, w0, w1, w2, a0, a1, a2, g1, g2, k_k, k_a, r_k,
                w_r, w_k, w_v, w_o, lnx_g, lnx_b):
    b, s, d = x3.shape
    bf = lambda w: w.astype(BF16)
    lora = lambda w: -(-w // LANES) * LANES
    vecs = jnp.stack([w0, a0, k_k, k_a, r_k.reshape(d), jnp.zeros_like(w0),
                      jnp.zeros_like(w0), jnp.zeros_like(w0)], axis=0).astype(F32)
    dw, da, dg = lora(w1.shape[1]), lora(a1.shape[1]), lora(g1.shape[1])
    r, ld, k, v, na, bb, gate, bonus = _rwkv_in(
        x3, mix_g, mix, vecs, bf(w_r), bf(w_k), bf(w_v),
        bf(_pad_cols(w1, dw)), bf(_pad_rows(w2, dw)),
        bf(_pad_cols(a1, da)), bf(_pad_rows(a2, da)),
        bf(_pad_cols(g1, dg)), bf(_pad_rows(g2, dg)))
    y = _scan(r, ld, k, v, na, bb)
    flat = lambda t: t.reshape(b * s, d)
    out = _rwkv_out(flat(x3), flat(y), flat(bonus), flat(gate), lnx_g, lnx_b, bf(w_o))
    return out.reshape(b, s, d)


def kernel(x, ffn_norm, ffn_w_gate, ffn_w_up, ffn_w_down, mix_norm, rel_bias, attn_w_in, attn_q_norm, attn_k_norm, attn_w_out, rw_mix, rw_w0, rw_w1, rw_w2, rw_a0, rw_a1, rw_a2, rw_g1, rw_g2, rw_kk, rw_ka, rw_rk, rw_wr, rw_wk, rw_wv, rw_wo, rw_lnx_g, rw_lnx_b):
    b, s, d = x.shape
    depth = ffn_norm.shape[0]

    def ffn(x3, layer, half):
        y = _ffn(x3.reshape(b * s, d), ffn_norm[layer, half],
                 ffn_w_gate[layer, half].astype(BF16), ffn_w_up[layer, half].astype(BF16),
                 ffn_w_down[layer, half].astype(BF16))
        return y.reshape(b, s, d)

    for layer in range(depth):
        x = ffn(x, layer, 0)
        if layer % 2 == 0:
            e = layer // 2
            x = _attention_layer(x, mix_norm[layer], rel_bias, attn_w_in[e], attn_q_norm[e],
                                 attn_k_norm[e], attn_w_out[e])
        else:
            o = layer // 2
            x = _rwkv_layer(x, mix_norm[layer], rw_mix[o], rw_w0[o], rw_w1[o], rw_w2[o],
                            rw_a0[o], rw_a1[o], rw_a2[o], rw_g1[o], rw_g2[o], rw_kk[o],
                            rw_ka[o], rw_rk[o], rw_wr[o], rw_wk[o], rw_wv[o], rw_wo[o],
                            rw_lnx_g[o], rw_lnx_b[o])
        x = ffn(x, layer, 1)
    return x
```

```python
import functools
import math

import jax
import jax.numpy as jnp
from jax import lax
from jax.experimental import pallas as pl
from jax.experimental.pallas import tpu as pltpu

F32 = jnp.float32
BF16 = jnp.bfloat16

HEAD_DIM = 64
LANES = 128
SUBLANES = 8
SB_HEADS = 4
DIL_HEADS = 12
DIL_PATTERNS = ((128, 1), (512, 4), (2048, 16))
DIL_GROUP_HEADS = 4
ATT_BLOCK = 128
N_BUCKETS = 32
MAX_DISTANCE = 2048
NORM_EPS = 1e-6
GN_EPS = 64e-5
NEG_INF = -1e30
LOG2E = math.log2(math.e)
SCAN_CHUNK = 64
VMEM_LIMIT = 56 * 1024 * 1024


def _params(*sem):
    return pltpu.CompilerParams(dimension_semantics=sem, vmem_limit_bytes=VMEM_LIMIT)


def _resident(shape):
    nd = len(shape)
    return pl.BlockSpec(shape, lambda *_: (0,) * nd, pipeline_mode=pl.Buffered(1))


def _rms(x, g):
    ms = jnp.mean(x * x, axis=-1, keepdims=True)
    return x * lax.rsqrt(ms + NORM_EPS) * g


def _dot(a, b):
    return jnp.dot(a, b, preferred_element_type=F32)


def _dot_nt(a, b):
    return lax.dot_general(a, b, (((1,), (1,)), ((), ())), preferred_element_type=F32)


def _split_bf16(x):
    hi = x.astype(BF16)
    lo = (x - hi.astype(F32)).astype(BF16)
    return hi, lo


def _head_ones():
    r = lax.broadcasted_iota(jnp.int32, (LANES, LANES), 0) // HEAD_DIM
    c = lax.broadcasted_iota(jnp.int32, (LANES, LANES), 1) // HEAD_DIM
    return (r == c).astype(BF16)


def _head_sum(x, ones, exact):
    outs = []
    for c in range(x.shape[1] // LANES):
        xc = x[:, c * LANES:(c + 1) * LANES]
        if exact:
            hi, lo = _split_bf16(xc)
            outs.append(_dot(hi, ones) + _dot(lo, ones))
        else:
            outs.append(_dot(xc.astype(BF16), ones))
    return outs[0] if len(outs) == 1 else jnp.concatenate(outs, axis=1)


FFN_TM = 512
FFN_FK = 256


def _ffn_apply(x, g_ref, wg_ref, wu_ref, wd_ref):
    h = _rms(x, g_ref[...]).astype(BF16)
    d_ff = wg_ref.shape[1]
    acc = jnp.zeros(x.shape, F32)
    for c in range(d_ff // FFN_FK):
        sl = slice(c * FFN_FK, (c + 1) * FFN_FK)
        gate = _dot(h, wg_ref[:, sl])
        up = _dot(h, wu_ref[:, sl])
        act = (gate * jax.nn.sigmoid(gate) * up).astype(BF16)
        acc = acc + _dot(act, wd_ref[sl, :])
    return x + 0.5 * acc


def _ffn_specs(ffn):
    return [_resident(a.shape) for a in ffn]


def _ffn_kernel(x_ref, g_ref, wg_ref, wu_ref, wd_ref, o_ref):
    o_ref[...] = _ffn_apply(x_ref[...], g_ref, wg_ref, wu_ref, wd_ref)


def _ffn(x2, *ffn):
    m, d = x2.shape
    tm = min(FFN_TM, m)
    return pl.pallas_call(
        _ffn_kernel,
        out_shape=jax.ShapeDtypeStruct((m, d), F32),
        grid=(m // tm,),
        in_specs=[pl.BlockSpec((tm, d), lambda i: (i, 0))] + _ffn_specs(ffn),
        out_specs=pl.BlockSpec((tm, d), lambda i: (i, 0)),
        compiler_params=_params("parallel"),
        name="ffn",
    )(x2, *ffn)


PROJ_TM = 512
PROJ_NC = 256
SB_COLS = 3 * SB_HEADS * HEAD_DIM
DIL_COLS = DIL_HEADS * HEAD_DIM


GROUP_COLS = 3 * DIL_GROUP_HEADS * HEAD_DIM
DILATIONS = tuple(r for _, r in DIL_PATTERNS)
assert PROJ_NC == DIL_GROUP_HEADS * HEAD_DIM


def _proj_kernel(x_ref, g_ref, w_ref, qn_ref, kn_ref, nat_ref, *rest):
    stream_refs, pt_ref = rest[:-1], rest[-1]
    h = _rms(x_ref[...], g_ref[...]).astype(BF16)
    tm = x_ref.shape[0]
    ones = _head_ones()
    scale = HEAD_DIM ** -0.5
    streams = dict(zip([g for g, r in enumerate(DILATIONS) if r > 1], stream_refs))
    n_chunks = w_ref.shape[1] // PROJ_NC
    cols = lambda c: slice(c * PROJ_NC, (c + 1) * PROJ_NC)
    group_of = lambda c: c // 3 - 1

    def finish(c, p):
        part, grp = c % 3, group_of(c)
        if grp >= 0 and part < 2:
            norm_gain = qn_ref[...] if part == 0 else kn_ref[...]
            ms = _head_sum(p * p, ones, exact=False) * (1.0 / HEAD_DIM)
            p = p * lax.rsqrt(ms + NORM_EPS) * norm_gain
        if part == 0:
            p = p * (scale * LOG2E if grp < 0 else scale)
        if grp not in streams:
            nat_ref[:, cols(c)] = p.astype(BF16)
            return
        r = DILATIONS[grp]
        n = tm // r
        for j in range(PROJ_NC // LANES):
            buf = pt_ref.at[c % 2, j]
            buf[...] = p[:, j * LANES:(j + 1) * LANES]
            lanes = slice(part * PROJ_NC + j * LANES, part * PROJ_NC + (j + 1) * LANES)
            for cc in range(r):
                streams[grp][cc, :, lanes] = buf[pl.ds(cc, n, stride=r), :].astype(BF16)

    product = lambda c: _dot(h, w_ref[:, cols(c)])
    pending = product(0)
    for c in range(1, n_chunks):
        nxt = product(c)
        finish(c - 1, pending)
        pending = nxt
    finish(n_chunks - 1, pending)


def _proj(x3, g, w_in, q_norm, k_norm):
    b, s, d = x3.shape
    tm = min(PROJ_TM, s)
    tiles = s // tm
    assert all(DILATIONS[g] == 1 for g in range(len(DILATIONS)) if g < DILATIONS.count(1))
    group_w = DIL_GROUP_HEADS * HEAD_DIM
    pieces = [w_in[:, :SB_COLS]] + [
        w_in[:, SB_COLS + part * DIL_COLS + g * group_w:][:, :group_w]
        for g in range(len(DILATIONS)) for part in range(3)]
    w = jnp.concatenate(pieces, axis=1).astype(BF16)
    nat_w = SB_COLS + GROUP_COLS * DILATIONS.count(1)
    tile = lambda v: jnp.tile(v.astype(F32), PROJ_NC // HEAD_DIM).reshape(1, PROJ_NC)
    stream_rs = [r for r in DILATIONS if r > 1]
    outs = pl.pallas_call(
        _proj_kernel,
        out_shape=(jax.ShapeDtypeStruct((b, s, nat_w), BF16),)
        + tuple(jax.ShapeDtypeStruct((b, r, s // r, GROUP_COLS), BF16) for r in stream_rs),
        grid=(b, tiles),
        in_specs=[
            pl.BlockSpec((None, tm, d), lambda bi, t: (bi, t, 0)),
            _resident((1, d)),
            _resident(w.shape),
            _resident((1, PROJ_NC)),
            _resident((1, PROJ_NC)),
        ],
        out_specs=(pl.BlockSpec((None, tm, nat_w), lambda bi, t: (bi, t, 0)),)
        + tuple(pl.BlockSpec((None, r, tm // r, GROUP_COLS), lambda bi, t: (bi, 0, t, 0))
                for r in stream_rs),
        scratch_shapes=[pltpu.VMEM((2, PROJ_NC // LANES, tm, LANES), F32)],
        compiler_params=_params("parallel", "parallel"),
        name="attn_proj",
    )(x3, g.reshape(1, d), w, tile(q_norm), tile(k_norm))
    return outs[0], list(outs[1:])


SB_TQ = 512
SB_UNROLL = 4


def _sb_kernel(q_ref, k_ref, v_ref, o_ref, qh_ref, acc_ref, run_ref):
    qi = pl.program_id(2)
    tq = q_ref.shape[0]
    tk = ATT_BLOCK
    nsub = tq // tk
    lane = lax.broadcasted_iota(jnp.int32, (1, LANES), 1)
    head0 = lane < HEAD_DIM
    q = q_ref[...]
    zero = jnp.zeros_like(q)
    qh_ref[0] = jnp.where(head0, q, zero)
    qh_ref[1] = jnp.where(head0, zero, q)
    acc_ref[...] = jnp.zeros_like(acc_ref)
    run_ref[...] = jnp.zeros_like(run_ref)
    later = (lax.broadcasted_iota(jnp.int32, (tk, tk), 0)
             > lax.broadcasted_iota(jnp.int32, (tk, tk), 1)).astype(BF16)

    def step(blocks, masked):
        starts = [pl.multiple_of(j * tk, tk) for j, _ in blocks]
        kbs = [k_ref[pl.ds(st, tk), :] for st in starts]
        vbs = [v_ref[pl.ds(st, tk), :] for st in starts]
        chains = [(b, h) for b in range(len(blocks)) for h in range(2)]
        zs = [_dot_nt(qh_ref[h, blocks[b][1]:, :], kbs[b]) for b, h in chains]
        log_betas = [jnp.minimum(z, 0.0) - jnp.log(1.0 + jnp.exp2(-jnp.abs(z))) * LOG2E for z in zs]
        log_keeps = [lb - z for lb, z in zip(log_betas, zs)]
        if masked:
            stricts = [lax.broadcasted_iota(jnp.int32, z.shape, 1)
                       < lax.broadcasted_iota(jnp.int32, z.shape, 0) for z in zs]
            log_keeps = [jnp.where(s, lk, 0.0) for s, lk in zip(stricts, log_keeps)]
        afters = [_dot(lk.astype(BF16), later) for lk in log_keeps]
        totals = [jnp.sum(lk, axis=1, keepdims=True) for lk in log_keeps]
        same_rows = len({r0 for _, r0 in blocks}) == 1
        for h in range(2):
            ws = []
            for b, (_, r0) in enumerate(blocks):
                c = 2 * b + h
                run = run_ref[h, r0:, :]
                w = jnp.exp2(log_betas[c] + afters[c] + run)
                if masked:
                    w = jnp.where(stricts[c], w, 0.0)
                ws.append(w.astype(BF16))
                run_ref[h, r0:, :] = run + totals[c]
                if not same_rows:
                    acc_ref[h, r0:, :] += _dot(ws[-1], vbs[b])
            if same_rows:
                r0 = blocks[0][1]
                acc_ref[h, r0:, :] += _dot(jnp.concatenate(ws, axis=1), jnp.concatenate(vbs, axis=0))

    step([(qi * nsub + c, c * tk) for c in reversed(range(nsub))], True)

    def body(i, carry):
        j = qi * nsub - 1 - SB_UNROLL * i
        step([(j - u, 0) for u in range(SB_UNROLL)], False)
        return carry

    lax.fori_loop(0, qi * (nsub // SB_UNROLL), body, 0)
    o_ref[...] = jnp.where(head0, acc_ref[0], acc_ref[1]).astype(o_ref.dtype)


def _sb_attention(qkv):
    b, s, _ = qkv.shape
    pairs = SB_HEADS * HEAD_DIM // LANES
    tq = min(SB_TQ, s)
    return pl.pallas_call(
        _sb_kernel,
        out_shape=jax.ShapeDtypeStruct((b, s, SB_HEADS * HEAD_DIM), BF16),
        grid=(b, pairs, s // tq),
        in_specs=[
            pl.BlockSpec((None, tq, LANES), lambda bi, p, i: (bi, i, p)),
            pl.BlockSpec((None, s, LANES), lambda bi, p, i: (bi, 0, pairs + p)),
            pl.BlockSpec((None, s, LANES), lambda bi, p, i: (bi, 0, 2 * pairs + p)),
        ],
        out_specs=pl.BlockSpec((None, tq, LANES), lambda bi, p, i: (bi, i, p)),
        scratch_shapes=[pltpu.VMEM((2, tq, LANES), BF16), pltpu.VMEM((2, tq, LANES), F32),
                        pltpu.VMEM((2, tq, LANES), F32)],
        compiler_params=_params("parallel", "parallel", "arbitrary"),
        name="sb_attn",
    )(qkv, qkv, qkv)


def _t5_bucket(dist):
    max_exact = N_BUCKETS // 2
    d = jnp.maximum(dist, 1).astype(F32)
    large = max_exact + (jnp.log(d / max_exact) / math.log(MAX_DISTANCE / max_exact)
                         * (N_BUCKETS - max_exact)).astype(jnp.int32)
    large = jnp.minimum(large, N_BUCKETS - 1)
    return jnp.where(dist < max_exact, dist, large)


def _bias_blocks(rel_bias_group, dilation):
    qi = jnp.arange(ATT_BLOCK)[:, None]
    kj = jnp.arange(2 * ATT_BLOCK)[None, :] - ATT_BLOCK
    dist = qi - kj
    bucket = _t5_bucket(jnp.maximum(dist, 0) * dilation)
    onehot = (bucket[None] == jnp.arange(N_BUCKETS)[:, None, None]).astype(F32)
    bias = jnp.einsum('nh,nqk->hqk', rel_bias_group.astype(F32), onehot,
                      precision=lax.Precision.HIGHEST)
    return bias.reshape(2, 2, ATT_BLOCK, 2 * ATT_BLOCK)


DIL_ROWS = 512


def _dil_kernel(q_ref, kp_ref, kc_ref, vp_ref, vc_ref, bias_ref, o_ref, lse_ref):
    n = pl.program_id(2)
    blk = ATT_BLOCK
    nb = q_ref.shape[0] // blk
    pairs = q_ref.shape[1] // LANES
    lane = lax.broadcasted_iota(jnp.int32, (1, LANES), 1)
    head0 = lane < HEAD_DIM
    row = lax.broadcasted_iota(jnp.int32, (blk, 2 * blk), 0)
    col = lax.broadcasted_iota(jnp.int32, (blk, 2 * blk), 1)
    dist = row - col + blk
    window = (dist >= 0) & (dist <= blk)
    first = window & ((n > 0) | (col >= blk))
    keys = jnp.concatenate([kp_ref[...], kc_ref[...]], axis=0)
    vals = jnp.concatenate([vp_ref[...], vc_ref[...]], axis=0)
    q = q_ref[...]
    zero = jnp.zeros_like(q)
    qh = (jnp.where(jnp.tile(head0, (1, pairs)), q, zero), jnp.where(jnp.tile(head0, (1, pairs)), zero, q))
    chains = [(i, p, h) for i in range(nb) for p in range(pairs) for h in range(2)]
    span = lambda i: slice(i * blk, (i + 2) * blk)
    lanes = lambda p: slice(p * LANES, (p + 1) * LANES)
    zs = [_dot_nt(qh[h][i * blk:(i + 1) * blk, lanes(p)], keys[span(i), lanes(p)])
          for i, p, h in chains]
    logits = [jnp.where(first if i == 0 else window, z + bias_ref[p, h], NEG_INF)
              for (i, p, h), z in zip(chains, zs)]
    ms = [jnp.max(l, axis=1, keepdims=True) for l in logits]
    ps = [jnp.exp(l - m) for l, m in zip(logits, ms)]
    dens = [jnp.sum(p_, axis=1, keepdims=True) for p_ in ps]
    outs = [_dot((p_ / den).astype(BF16), vals[span(i), lanes(p)])
            for (i, p, _), p_, den in zip(chains, ps, dens)]
    lses = [m + jnp.log(den) for m, den in zip(ms, dens)]
    for c in range(0, len(chains), 2):
        i, p, _ = chains[c]
        rows = slice(i * blk, (i + 1) * blk)
        o_ref[rows, lanes(p)] = jnp.where(head0, outs[c], outs[c + 1])
        lse_ref[rows, lanes(p)] = jnp.where(head0, lses[c], lses[c + 1])


def _dil_attention(src, col_blocks, bias):
    b, r, l, _ = src.shape
    width = DIL_GROUP_HEADS * HEAD_DIM
    blk = ATT_BLOCK
    rows = min(DIL_ROWS, l)
    nb = rows // blk
    assert all(off * LANES % width == 0 for off in col_blocks)
    qo, ko, vo = (off * LANES // width for off in col_blocks)
    cur = lambda off: (lambda bi, c, n: (bi, c, n, off))
    prev = lambda off: (lambda bi, c, n: (bi, c, jnp.maximum(n * nb - 1, 0), off))
    tile = (None, None, rows, width)
    one = (None, None, blk, width)
    out = jax.ShapeDtypeStruct((b, r, l, width), F32)
    return pl.pallas_call(
        _dil_kernel,
        out_shape=(out, out),
        grid=(b, r, l // rows),
        in_specs=[
            pl.BlockSpec(tile, cur(qo)),
            pl.BlockSpec(one, prev(ko)),
            pl.BlockSpec(tile, cur(ko)),
            pl.BlockSpec(one, prev(vo)),
            pl.BlockSpec(tile, cur(vo)),
            _resident(bias.shape),
        ],
        out_specs=(pl.BlockSpec(tile, cur(0)), pl.BlockSpec(tile, cur(0))),
        compiler_params=_params("parallel", "parallel", "arbitrary"),
        name="dil_attn",
    )(src, src, src, src, src, bias)


OUT_TM = 512


def _attn_out_kernel(x_ref, sb_ref, o0_ref, o1_ref, o2_ref, l0_ref, l1_ref, l2_ref, w_ref,
                     g_ref, wg_ref, wu_ref, wd_ref, y_ref, *order_refs):
    tm = x_ref.shape[0]
    scratch = list(order_refs)

    def natural(ref):
        r = ref.shape[0]
        if r == 1:
            return ref[0]
        buf = scratch.pop()
        for c in range(r):
            for j in range(buf.shape[0]):
                buf[j, pl.ds(c, tm // r, stride=r), :] = ref[c, :, j * LANES:(j + 1) * LANES]
        return jnp.concatenate([buf[j] for j in range(buf.shape[0])], axis=1)

    l0, l1, l2 = natural(l0_ref), natural(l1_ref), natural(l2_ref)
    mx = jnp.maximum(jnp.maximum(l0, l1), l2)
    e0, e1, e2 = jnp.exp(l0 - mx), jnp.exp(l1 - mx), jnp.exp(l2 - mx)
    den = e0 + e1 + e2
    out_b = ((e0 / den) * natural(o0_ref) + (e1 / den) * natural(o1_ref)
             + (e2 / den) * natural(o2_ref))
    na = sb_ref.shape[1]
    y = _dot(sb_ref[...], w_ref[:na, :]) + _dot(out_b.astype(BF16), w_ref[na:, :])
    y_ref[...] = _ffn_apply(x_ref[...] + y, g_ref, wg_ref, wu_ref, wd_ref)


def _attn_out(x3, sb, outs, lses, w_out, ffn):
    b, s, d = x3.shape
    tm = min(OUT_TM, s)
    wide = sb.shape[2]
    row = lambda w: pl.BlockSpec((None, tm, w), lambda bi, t: (bi, t, 0))
    stream = lambda a: pl.BlockSpec((None, a.shape[1], tm // a.shape[1], wide),
                                    lambda bi, t: (bi, 0, t, 0))
    n_buffers = 2 * sum(1 for a in outs if a.shape[1] > 1)
    return pl.pallas_call(
        _attn_out_kernel,
        out_shape=jax.ShapeDtypeStruct((b, s, d), F32),
        grid=(b, s // tm),
        in_specs=[row(d), row(wide)] + [stream(a) for a in (*outs, *lses)]
        + [_resident(w_out.shape)] + _ffn_specs(ffn),
        out_specs=row(d),
        scratch_shapes=[pltpu.VMEM((wide // LANES, tm, LANES), F32)] * n_buffers,
        compiler_params=_params("parallel", "parallel"),
        name="attn_out_ffn",
    )(x3, sb, *outs, *lses, w_out, *ffn)


RW_TS = 512


def _rwkv_in_kernel(x_ref, xp_ref, g_ref, mix_ref, vec_ref, wr_ref, wk_ref, wv_ref,
                    w1_ref, w2_ref, a1_ref, a2_ref, g1_ref, g2_ref,
                    r_ref, ld_ref, k_ref, v_ref, na_ref, bb_ref, gate_ref, bonus_ref):
    si = pl.program_id(1)
    gain = g_ref[...]
    h = _rms(x_ref[...], gain)
    prev_last = _rms(xp_ref[...], gain)[SUBLANES - 1:, :] * (si > 0).astype(F32)
    rows = lax.broadcasted_iota(jnp.int32, h.shape, 0)
    shifted = jnp.where(rows == 0, prev_last, pltpu.roll(h, 1, axis=0))
    xx = shifted - h
    mixed = lambda i: (h + xx * mix_ref[i:i + 1, :]).astype(BF16)
    w0, a0, k_k, k_a, r_k = (vec_ref[i:i + 1, :] for i in range(5))

    r = _dot(mixed(0), wr_ref[...])
    k = _dot(mixed(2), wk_ref[...])
    v = _dot(mixed(3), wv_ref[...])
    lora_w = _dot(jnp.tanh(_dot(mixed(1), w1_ref[...])).astype(BF16), w2_ref[...])
    lora_a = _dot(_dot(mixed(4), a1_ref[...]).astype(BF16), a2_ref[...])
    gate = _dot(jax.nn.sigmoid(_dot(mixed(5), g1_ref[...])).astype(BF16), g2_ref[...])

    t = -(w0 + lora_w)
    w_log = -(jnp.maximum(t, 0.0) + jnp.log(1.0 + jnp.exp(-jnp.abs(t)))) - 0.5
    a = jax.nn.sigmoid(a0 + lora_a)
    ones = _head_ones()
    kk = k * k_k
    norm = jnp.sqrt(_head_sum(kk * kk, ones, exact=False))
    kk = kk / jnp.maximum(norm, 1e-12)
    k = k * (1.0 + (a - 1.0) * k_a)

    r_ref[...] = r.astype(r_ref.dtype)
    ld_ref[...] = -jnp.exp(w_log)
    k_ref[...] = k.astype(k_ref.dtype)
    v_ref[...] = v.astype(v_ref.dtype)
    na_ref[...] = (-kk).astype(na_ref.dtype)
    bb_ref[...] = (kk * a).astype(bb_ref.dtype)
    gate_ref[...] = gate.astype(gate_ref.dtype)
    bonus_ref[...] = _head_sum(r * k * r_k, ones, exact=True) * v


def _rwkv_in(x3, g, mix, vecs, wr, wk, wv, w1, w2, a1, a2, g1, g2):
    b, s, d = x3.shape
    ts = min(RW_TS, s)
    tile = pl.BlockSpec((None, ts, d), lambda bi, si: (bi, si, 0))
    prev = pl.BlockSpec((None, SUBLANES, d),
                        lambda bi, si: (bi, jnp.maximum(si * (ts // SUBLANES) - 1, 0), 0))
    out = lambda dt: jax.ShapeDtypeStruct((b, s, d), dt)
    weights = (wr, wk, wv, w1, w2, a1, a2, g1, g2)
    return pl.pallas_call(
        _rwkv_in_kernel,
        out_shape=(out(BF16), out(F32), out(BF16), out(BF16), out(BF16), out(BF16), out(BF16),
                   out(F32)),
        grid=(b, s // ts),
        in_specs=[tile, prev, _resident((1, d)), _resident(mix.shape), _resident(vecs.shape)]
        + [_resident(w.shape) for w in weights],
        out_specs=(tile,) * 8,
        compiler_params=_params("parallel", "arbitrary"),
        name="rwkv_in",
    )(x3, x3, g.reshape(1, d), mix, vecs, *weights)


SCAN_ROWS = 1024


def _scan_kernel(tiles_per_seq, r_ref, ld_ref, k_ref, v_ref, na_ref, bb_ref, y_ref,
                 state_ref, qm_ref, y3_ref, n_ref):
    c_len = SCAN_CHUNK
    two = 2 * c_len
    n_chunks = r_ref.shape[0] // c_len
    step = pl.program_id(0)

    @pl.when(step == 0)
    def _():
        state_ref[...] = jnp.zeros_like(state_ref)
        qm_ref[...] = jnp.zeros_like(qm_ref)
        y3_ref[...] = jnp.zeros_like(y3_ref)
        n_ref[...] = jnp.zeros_like(n_ref)

    lane = lax.broadcasted_iota(jnp.int32, (1, LANES), 1)
    head0 = lane < HEAD_DIM
    ti = lax.broadcasted_iota(jnp.int32, (c_len, two), 0)
    ii = lax.broadcasted_iota(jnp.int32, (c_len, two), 1) % c_len
    strict, incl, ident = ti > ii, ti >= ii, ti == ii
    ri = lax.broadcasted_iota(jnp.int32, (LANES, LANES), 0)
    ci = lax.broadcasted_iota(jnp.int32, (LANES, LANES), 1)
    same_head = (ri // HEAD_DIM) == (ci // HEAD_DIM)
    eye = ri == ci
    tri = (lax.broadcasted_iota(jnp.int32, (c_len, c_len), 0)
           >= lax.broadcasted_iota(jnp.int32, (c_len, c_len), 1)).astype(BF16)
    tri2 = jnp.concatenate([tri, tri], axis=1)

    def stack(x):
        zero = jnp.zeros_like(x)
        return jnp.concatenate([jnp.where(head0, x, zero), jnp.where(head0, zero, x)], axis=0)

    bf = lambda x: x.astype(BF16)
    each = lambda f, *ls: [f(*xs) for xs in zip(*ls)]

    def prepare(rw, ld, cum):
        r, k, v, na, bb = (ref[rw, :].astype(F32) for ref in (r_ref, k_ref, v_ref, na_ref, bb_ref))
        total = cum[c_len - 1:, :]
        e_neg = jnp.exp(-cum)
        e_tail = jnp.exp(total - cum)
        a_t = bf(na * jnp.exp(cum - ld))
        r_t = r * jnp.exp(cum)
        vb = bf(v)
        tails = jnp.concatenate([bb * e_tail, k * e_tail], axis=0)
        return dict(
            r_t=r_t, a_st=stack(a_t), vb=vb, v_st=stack(vb),
            lhs=jnp.concatenate([a_t, bf(r_t)], axis=0),
            rhs=jnp.concatenate([stack(bf(bb * e_neg)), stack(bf(k * e_neg))], axis=0),
            tails_t=bf(jnp.transpose(tails)),
            decay=jnp.where(eye, jnp.exp(total), 0.0))

    def transitions(rows, tick):
        n = len(rows)
        lds = [ld_ref[rw, :] for rw in rows]
        parts = each(_split_bf16, lds)
        his = jnp.concatenate([p[0] for p in parts], axis=1) if n > 1 else parts[0][0]
        los = jnp.concatenate([p[1] for p in parts], axis=1) if n > 1 else parts[0][1]
        cum_all = _dot(tri2, jnp.concatenate([his, los], axis=0))
        cums = [cum_all[:, g * LANES:(g + 1) * LANES] for g in range(n)]
        ps = each(prepare, rows, lds, cums)
        tick()

        grams = [_dot_nt(p["lhs"], p["rhs"]) for p in ps]
        n_abs = [jnp.where(strict, g[:c_len, :two], 0.0) for g in grams]
        t1s = [_dot(bf(jnp.where(strict, g[:c_len, two:], 0.0)), p["v_st"])
               for g, p in zip(grams, ps)]
        b_rs = [bf(jnp.concatenate([jnp.where(incl, g[c_len:, :two], 0.0),
                                    jnp.where(incl, g[c_len:, two:], 0.0)], axis=1)) for g in grams]
        tick()

        pws = [_dot(bf(x), stack(bf(x))) for x in n_abs]
        invs = [jnp.where(ident, 1.0, x) for x in n_abs]
        tick()
        terms = 2
        while terms < c_len:
            last = 2 * terms >= c_len
            nxt_p, nxt_t = [], []
            for pw, inv in zip(pws, invs):
                pwb = bf(pw)
                if last:
                    nxt_t.append(inv + _dot(pwb, stack(bf(inv))))
                else:
                    both = _dot(pwb, jnp.concatenate([stack(pwb), stack(bf(inv))], axis=1))
                    nxt_p.append(both[:, :two])
                    nxt_t.append(inv + both[:, two:])
            pws, invs = nxt_p, nxt_t
            terms *= 2
            tick()

        wus = [bf(_dot(bf(inv), jnp.concatenate([p["a_st"], stack(bf(t1))], axis=1)))
               for inv, p, t1 in zip(invs, ps, t1s)]
        tick()
        zeros = jnp.zeros((c_len, LANES), BF16)
        tops = [_dot(p["tails_t"], jnp.concatenate(
                    [wu, jnp.concatenate([zeros, p["vb"]], axis=1)], axis=0))
                for p, wu in zip(ps, wus)]
        lows = [_dot(b_r, jnp.concatenate(
                    [jnp.concatenate([stack(wu[:, :LANES]), stack(wu[:, LANES:])], axis=1),
                     jnp.concatenate([jnp.zeros((two, LANES), BF16), p["v_st"]], axis=1)], axis=0))
                for b_r, p, wu in zip(b_rs, ps, wus)]
        out = []
        for p, top, low in zip(ps, tops, lows):
            m_mat = p["decay"] + jnp.where(same_head, top[:, :LANES], 0.0)
            n_mat = jnp.where(same_head, top[:, LANES:], 0.0)
            q2 = p["r_t"] + low[:, :LANES]
            out.append((bf(jnp.concatenate([q2, m_mat], axis=0)), low[:, LANES:], n_mat))
        return out

    rows = [pl.ds(g * c_len, c_len) for g in range(n_chunks)]
    starts_seq = (step - 1) % tiles_per_seq == 0
    chain = {"state": jnp.where(starts_seq, 0.0, state_ref[...]), "next": 0}

    def chain_step():
        g = chain["next"]
        if g == n_chunks:
            return
        chain["next"] = g + 1
        res = _dot(qm_ref[g], bf(chain["state"]))
        y_ref[rows[g], :] = res[:c_len] + y3_ref[g]
        chain["state"] = res[c_len:] + n_ref[g]

    trans = transitions(rows, chain_step)
    while chain["next"] < n_chunks:
        chain_step()
    state_ref[...] = chain["state"]
    for g, (qm, y3, n_mat) in enumerate(trans):
        qm_ref[g] = qm
        y3_ref[g] = y3
        n_ref[g] = n_mat


def _scan(r, ld, k, v, na, bb):
    b, s, d = r.shape
    rows = min(SCAN_ROWS, s)
    assert s % rows == 0 and rows % SCAN_CHUNK == 0
    pairs, tiles = d // LANES, s // rows
    n_tiles = b * pairs * tiles
    n_chunks = rows // SCAN_CHUNK

    def tile_of(j):
        return j // (pairs * tiles), j % tiles, (j // tiles) % pairs

    in_tile = pl.BlockSpec((None, rows, LANES), lambda i: tile_of(jnp.minimum(i, n_tiles - 1)))
    out_tile = pl.BlockSpec((None, rows, LANES), lambda i: tile_of(jnp.maximum(i - 1, 0)))
    return pl.pallas_call(
        functools.partial(_scan_kernel, tiles),
        out_shape=jax.ShapeDtypeStruct((b, s, d), F32),
        grid=(n_tiles + 1,),
        in_specs=[in_tile] * 6,
        out_specs=out_tile,
        scratch_shapes=[pltpu.VMEM((LANES, LANES), F32),
                        pltpu.VMEM((n_chunks, SCAN_CHUNK + LANES, LANES), BF16),
                        pltpu.VMEM((n_chunks, SCAN_CHUNK, LANES), F32),
                        pltpu.VMEM((n_chunks, LANES, LANES), F32)],
        compiler_params=_params("arbitrary"),
        name="rwkv_scan",
    )(r, ld, k, v, na, bb)


RW_OUT_TM = 512


def _rwkv_out_kernel(x_ref, y_ref, bonus_ref, gate_ref, lg_ref, lb_ref, wo_ref,
                     g_ref, wg_ref, wu_ref, wd_ref, o_ref):
    y = y_ref[...]
    ones = _head_ones()
    mu = _head_sum(y, ones, exact=True) * (1.0 / HEAD_DIM)
    dlt = y - mu
    var = _head_sum(dlt * dlt, ones, exact=True) * (1.0 / HEAD_DIM)
    yn = dlt * lax.rsqrt(var + GN_EPS) * lg_ref[...] + lb_ref[...] + bonus_ref[...]
    x = x_ref[...] + _dot((yn * gate_ref[...].astype(F32)).astype(BF16), wo_ref[...])
    o_ref[...] = _ffn_apply(x, g_ref, wg_ref, wu_ref, wd_ref)


def _rwkv_out(x2, y2, bonus2, gate2, lnx_g, lnx_b, wo, ffn):
    m, d = x2.shape
    tm = min(RW_OUT_TM, m)
    row = pl.BlockSpec((tm, d), lambda i: (i, 0))
    return pl.pallas_call(
        _rwkv_out_kernel,
        out_shape=jax.ShapeDtypeStruct((m, d), F32),
        grid=(m // tm,),
        in_specs=[row, row, row, row, _resident((1, d)), _resident((1, d)), _resident(wo.shape)]
        + _ffn_specs(ffn),
        out_specs=row,
        compiler_params=_params("parallel"),
        name="rwkv_out_ffn",
    )(x2, y2, bonus2, gate2, lnx_g.reshape(1, d), lnx_b.reshape(1, d), wo, *ffn)


def _attention_layer(x3, mix_g, rel_bias, w_in, q_norm, k_norm, w_out, ffn):
    b, s, d = x3.shape
    nat, streams = _proj(x3, mix_g, w_in, q_norm, k_norm)
    sb = _sb_attention(nat)
    outs, lses = [], []
    group_w = DIL_GROUP_HEADS * HEAD_DIM
    for g, r in enumerate(DILATIONS):
        if r == 1:
            src = nat.reshape(b, 1, s, -1)
            base = SB_COLS + GROUP_COLS * g
        else:
            src, base = streams.pop(0), 0
        offs = tuple((base + part * group_w) // LANES for part in range(3))
        bias = _bias_blocks(rel_bias[:, g * DIL_GROUP_HEADS:(g + 1) * DIL_GROUP_HEADS], r)
        o, l = _dil_attention(src, offs, bias)
        outs.append(o)
        lses.append(l)
    return _attn_out(x3, sb, outs, lses, w_out.astype(BF16), ffn)


def _pad_cols(w, n):
    return jnp.pad(w, ((0, 0), (0, n - w.shape[1])))


def _pad_rows(w, n):
    return jnp.pad(w, ((0, n - w.shape[0]), (0, 0)))


def _rwkv_layer(x3, mix_g, mix, w0, w1, w2, a0, a1, a2, g1, g2, k_k, k_a, r_k,
                w_r, w_k, w_v, w_o, lnx_g, lnx_b, ffn):
    b, s, d = x3.shape
    bf = lambda w: w.astype(BF16)
    lora = lambda w: -(-w // LANES) * LANES
    vecs = jnp.stack([w0, a0, k_k, k_a, r_k.reshape(d), jnp.zeros_like(w0),
                      jnp.zeros_like(w0), jnp.zeros_like(w0)], axis=0).astype(F32)
    dw, da, dg = lora(w1.shape[1]), lora(a1.shape[1]), lora(g1.shape[1])
    r, ld, k, v, na, bb, gate, bonus = _rwkv_in(
        x3, mix_g, mix, vecs, bf(w_r), bf(w_k), bf(w_v),
        bf(_pad_cols(w1, dw)), bf(_pad_rows(w2, dw)),
        bf(_pad_cols(a1, da)), bf(_pad_rows(a2, da)),
        bf(_pad_cols(g1, dg)), bf(_pad_rows(g2, dg)))
    y = _scan(r, ld, k, v, na, bb)
    flat = lambda t: t.reshape(b * s, d)
    out = _rwkv_out(flat(x3), flat(y), flat(bonus), flat(gate), lnx_g, lnx_b, bf(w_o), ffn)
    return out.reshape(b, s, d)


def kernel(x, ffn_norm, ffn_w_gate, ffn_w_up, ffn_w_down, mix_norm, rel_bias, attn_w_in, attn_q_norm, attn_k_norm, attn_w_out, rw_mix, rw_w0, rw_w1, rw_w2, rw_a0, rw_a1, rw_a2, rw_g1, rw_g2, rw_kk, rw_ka, rw_rk, rw_wr, rw_wk, rw_wv, rw_wo, rw_lnx_g, rw_lnx_b):
    b, s, d = x.shape
    depth = ffn_norm.shape[0]

    def ffn_operands(layer, half):
        return (ffn_norm[layer, half].reshape(1, d).astype(F32),
                ffn_w_gate[layer, half].astype(BF16), ffn_w_up[layer, half].astype(BF16),
                ffn_w_down[layer, half].astype(BF16))

    for layer in range(depth):
        x = _ffn(x.reshape(b * s, d), *ffn_operands(layer, 0)).reshape(b, s, d)
        second = ffn_operands(layer, 1)
        if layer % 2 == 0:
            e = layer // 2
            x = _attention_layer(x, mix_norm[layer], rel_bias, attn_w_in[e], attn_q_norm[e],
                                 attn_k_norm[e], attn_w_out[e], second)
        else:
            o = layer // 2
            x = _rwkv_layer(x, mix_norm[layer], rw_mix[o], rw_w0[o], rw_w1[o], rw_w2[o],
                            rw_a0[o], rw_a1[o], rw_a2[o], rw_g1[o], rw_g2[o], rw_kk[o],
                            rw_ka[o], rw_rk[o], rw_wr[o], rw_wk[o], rw_wv[o], rw_wo[o],
                            rw_lnx_g[o], rw_lnx_b[o], second)
    return x
```

```python
import functools
import math

import jax
import jax.numpy as jnp
from jax import lax
from jax.experimental import pallas as pl
from jax.experimental.pallas import tpu as pltpu

F32 = jnp.float32
BF16 = jnp.bfloat16

HEAD_DIM = 64
LANES = 128
SUBLANES = 8
SB_HEADS = 4
DIL_HEADS = 12
DIL_PATTERNS = ((128, 1), (512, 4), (2048, 16))
DIL_GROUP_HEADS = 4
ATT_BLOCK = 128
N_BUCKETS = 32
MAX_DISTANCE = 2048
NORM_EPS = 1e-6
GN_EPS = 64e-5
NEG_INF = -1e30
LOG2E = math.log2(math.e)
SCAN_CHUNK = 64
VMEM_LIMIT = 56 * 1024 * 1024


def _params(*sem):
    return pltpu.CompilerParams(dimension_semantics=sem, vmem_limit_bytes=VMEM_LIMIT)


def _resident(shape):
    nd = len(shape)
    return pl.BlockSpec(shape, lambda *_: (0,) * nd, pipeline_mode=pl.Buffered(1))


def _rms(x, g):
    ms = jnp.mean(x * x, axis=-1, keepdims=True)
    return x * lax.rsqrt(ms + NORM_EPS) * g


def _dot(a, b):
    return jnp.dot(a, b, preferred_element_type=F32)


def _dot_nt(a, b):
    return lax.dot_general(a, b, (((1,), (1,)), ((), ())), preferred_element_type=F32)


def _split_bf16(x):
    hi = x.astype(BF16)
    lo = (x - hi.astype(F32)).astype(BF16)
    return hi, lo


def _head_ones():
    r = lax.broadcasted_iota(jnp.int32, (LANES, LANES), 0) // HEAD_DIM
    c = lax.broadcasted_iota(jnp.int32, (LANES, LANES), 1) // HEAD_DIM
    return (r == c).astype(BF16)


def _head_sum(x, ones, exact):
    outs = []
    for c in range(x.shape[1] // LANES):
        xc = x[:, c * LANES:(c + 1) * LANES]
        if exact:
            hi, lo = _split_bf16(xc)
            outs.append(_dot(hi, ones) + _dot(lo, ones))
        else:
            outs.append(_dot(xc.astype(BF16), ones))
    return outs[0] if len(outs) == 1 else jnp.concatenate(outs, axis=1)


FFN_TM = 512
FFN_FK = 256


def _ffn_apply(x, g_ref, wg_ref, wu_ref, wd_ref):
    h = _rms(x, g_ref[...]).astype(BF16)
    d_ff = wg_ref.shape[1]
    acc = jnp.zeros(x.shape, F32)
    for c in range(d_ff // FFN_FK):
        sl = slice(c * FFN_FK, (c + 1) * FFN_FK)
        gate = _dot(h, wg_ref[:, sl])
        up = _dot(h, wu_ref[:, sl])
        act = (gate * jax.nn.sigmoid(gate) * up).astype(BF16)
        acc = acc + _dot(act, wd_ref[sl, :])
    return x + 0.5 * acc


def _ffn_specs(ffn):
    return [_resident(a.shape) for a in ffn]


def _ffn_kernel(x_ref, g_ref, wg_ref, wu_ref, wd_ref, o_ref):
    o_ref[...] = _ffn_apply(x_ref[...], g_ref, wg_ref, wu_ref, wd_ref)


def _ffn(x2, *ffn):
    m, d = x2.shape
    tm = min(FFN_TM, m)
    return pl.pallas_call(
        _ffn_kernel,
        out_shape=jax.ShapeDtypeStruct((m, d), F32),
        grid=(m // tm,),
        in_specs=[pl.BlockSpec((tm, d), lambda i: (i, 0))] + _ffn_specs(ffn),
        out_specs=pl.BlockSpec((tm, d), lambda i: (i, 0)),
        compiler_params=_params("parallel"),
        name="ffn",
    )(x2, *ffn)


PROJ_TM = 512
PROJ_NC = 256
SB_COLS = 3 * SB_HEADS * HEAD_DIM
DIL_COLS = DIL_HEADS * HEAD_DIM


GROUP_COLS = 3 * DIL_GROUP_HEADS * HEAD_DIM
DILATIONS = tuple(r for _, r in DIL_PATTERNS)
assert PROJ_NC == DIL_GROUP_HEADS * HEAD_DIM


def _proj_kernel(x_ref, g_ref, w_ref, qn_ref, kn_ref, nat_ref, *rest):
    stream_refs, pt_ref = rest[:-1], rest[-1]
    h = _rms(x_ref[...], g_ref[...]).astype(BF16)
    tm = x_ref.shape[0]
    ones = _head_ones()
    scale = HEAD_DIM ** -0.5
    streams = dict(zip([g for g, r in enumerate(DILATIONS) if r > 1], stream_refs))
    n_chunks = w_ref.shape[1] // PROJ_NC
    cols = lambda c: slice(c * PROJ_NC, (c + 1) * PROJ_NC)
    group_of = lambda c: c // 3 - 1

    def finish(c, p):
        part, grp = c % 3, group_of(c)
        if grp >= 0 and part < 2:
            norm_gain = qn_ref[...] if part == 0 else kn_ref[...]
            ms = _head_sum(p * p, ones, exact=False) * (1.0 / HEAD_DIM)
            p = p * lax.rsqrt(ms + NORM_EPS) * norm_gain
        if part == 0:
            p = p * (scale * LOG2E if grp < 0 else scale)
        if grp not in streams:
            nat_ref[:, cols(c)] = p.astype(BF16)
            return
        r = DILATIONS[grp]
        n = tm // r
        for j in range(PROJ_NC // LANES):
            buf = pt_ref.at[c % 2, j]
            buf[...] = p[:, j * LANES:(j + 1) * LANES]
            lanes = slice(part * PROJ_NC + j * LANES, part * PROJ_NC + (j + 1) * LANES)
            for cc in range(r):
                streams[grp][cc, :, lanes] = buf[pl.ds(cc, n, stride=r), :].astype(BF16)

    product = lambda c: _dot(h, w_ref[:, cols(c)])
    pending = product(0)
    for c in range(1, n_chunks):
        nxt = product(c)
        finish(c - 1, pending)
        pending = nxt
    finish(n_chunks - 1, pending)


def _proj(x3, g, w_in, q_norm, k_norm):
    b, s, d = x3.shape
    tm = min(PROJ_TM, s)
    tiles = s // tm
    assert all(DILATIONS[g] == 1 for g in range(len(DILATIONS)) if g < DILATIONS.count(1))
    group_w = DIL_GROUP_HEADS * HEAD_DIM
    pieces = [w_in[:, :SB_COLS]] + [
        w_in[:, SB_COLS + part * DIL_COLS + g * group_w:][:, :group_w]
        for g in range(len(DILATIONS)) for part in range(3)]
    w = jnp.concatenate(pieces, axis=1).astype(BF16)
    nat_w = SB_COLS + GROUP_COLS * DILATIONS.count(1)
    tile = lambda v: jnp.tile(v.astype(F32), PROJ_NC // HEAD_DIM).reshape(1, PROJ_NC)
    stream_rs = [r for r in DILATIONS if r > 1]
    outs = pl.pallas_call(
        _proj_kernel,
        out_shape=(jax.ShapeDtypeStruct((b, s, nat_w), BF16),)
        + tuple(jax.ShapeDtypeStruct((b, r, s // r, GROUP_COLS), BF16) for r in stream_rs),
        grid=(b, tiles),
        in_specs=[
            pl.BlockSpec((None, tm, d), lambda bi, t: (bi, t, 0)),
            _resident((1, d)),
            _resident(w.shape),
            _resident((1, PROJ_NC)),
            _resident((1, PROJ_NC)),
        ],
        out_specs=(pl.BlockSpec((None, tm, nat_w), lambda bi, t: (bi, t, 0)),)
        + tuple(pl.BlockSpec((None, r, tm // r, GROUP_COLS), lambda bi, t: (bi, 0, t, 0))
                for r in stream_rs),
        scratch_shapes=[pltpu.VMEM((2, PROJ_NC // LANES, tm, LANES), F32)],
        compiler_params=_params("parallel", "parallel"),
        name="attn_proj",
    )(x3, g.reshape(1, d), w, tile(q_norm), tile(k_norm))
    return outs[0], list(outs[1:])


SB_TQ = 512
SB_UNROLL = 4


def _sb_kernel(q_ref, k_ref, v_ref, o_ref, qh_ref, acc_ref, run_ref):
    qi = pl.program_id(2)
    tq = q_ref.shape[0]
    tk = ATT_BLOCK
    nsub = tq // tk
    lane = lax.broadcasted_iota(jnp.int32, (1, LANES), 1)
    head0 = lane < HEAD_DIM
    q = q_ref[...]
    zero = jnp.zeros_like(q)
    qh_ref[0] = jnp.where(head0, q, zero)
    qh_ref[1] = jnp.where(head0, zero, q)
    acc_ref[...] = jnp.zeros_like(acc_ref)
    run_ref[...] = jnp.zeros_like(run_ref)
    later = (lax.broadcasted_iota(jnp.int32, (tk, tk), 0)
             > lax.broadcasted_iota(jnp.int32, (tk, tk), 1)).astype(BF16)

    def step(blocks, masked):
        starts = [pl.multiple_of(j * tk, tk) for j, _ in blocks]
        kbs = [k_ref[pl.ds(st, tk), :] for st in starts]
        vbs = [v_ref[pl.ds(st, tk), :] for st in starts]
        chains = [(b, h) for b in range(len(blocks)) for h in range(2)]
        zs = [_dot_nt(qh_ref[h, blocks[b][1]:, :], kbs[b]) for b, h in chains]
        log_betas = [jnp.minimum(z, 0.0) - jnp.log(1.0 + jnp.exp2(-jnp.abs(z))) * LOG2E for z in zs]
        log_keeps = [lb - z for lb, z in zip(log_betas, zs)]
        if masked:
            stricts = [lax.broadcasted_iota(jnp.int32, z.shape, 1)
                       < lax.broadcasted_iota(jnp.int32, z.shape, 0) for z in zs]
            log_keeps = [jnp.where(s, lk, 0.0) for s, lk in zip(stricts, log_keeps)]
        afters = [_dot(lk.astype(BF16), later) for lk in log_keeps]
        totals = [jnp.sum(lk, axis=1, keepdims=True) for lk in log_keeps]
        same_rows = len({r0 for _, r0 in blocks}) == 1
        for h in range(2):
            ws = []
            for b, (_, r0) in enumerate(blocks):
                c = 2 * b + h
                run = run_ref[h, r0:, :]
                w = jnp.exp2(log_betas[c] + afters[c] + run)
                if masked:
                    w = jnp.where(stricts[c], w, 0.0)
                ws.append(w.astype(BF16))
                run_ref[h, r0:, :] = run + totals[c]
                if not same_rows:
                    acc_ref[h, r0:, :] += _dot(ws[-1], vbs[b])
            if same_rows:
                r0 = blocks[0][1]
                acc_ref[h, r0:, :] += _dot(jnp.concatenate(ws, axis=1), jnp.concatenate(vbs, axis=0))

    step([(qi * nsub + c, c * tk) for c in reversed(range(nsub))], True)

    def body(i, carry):
        j = qi * nsub - 1 - SB_UNROLL * i
        step([(j - u, 0) for u in range(SB_UNROLL)], False)
        return carry

    lax.fori_loop(0, qi * (nsub // SB_UNROLL), body, 0)
    o_ref[...] = jnp.where(head0, acc_ref[0], acc_ref[1]).astype(o_ref.dtype)


def _sb_attention(qkv):
    b, s, _ = qkv.shape
    pairs = SB_HEADS * HEAD_DIM // LANES
    tq = min(SB_TQ, s)
    return pl.pallas_call(
        _sb_kernel,
        out_shape=jax.ShapeDtypeStruct((b, s, SB_HEADS * HEAD_DIM), BF16),
        grid=(b, pairs, s // tq),
        in_specs=[
            pl.BlockSpec((None, tq, LANES), lambda bi, p, i: (bi, i, p)),
            pl.BlockSpec((None, s, LANES), lambda bi, p, i: (bi, 0, pairs + p)),
            pl.BlockSpec((None, s, LANES), lambda bi, p, i: (bi, 0, 2 * pairs + p)),
        ],
        out_specs=pl.BlockSpec((None, tq, LANES), lambda bi, p, i: (bi, i, p)),
        scratch_shapes=[pltpu.VMEM((2, tq, LANES), BF16), pltpu.VMEM((2, tq, LANES), F32),
                        pltpu.VMEM((2, tq, LANES), F32)],
        compiler_params=_params("parallel", "parallel", "arbitrary"),
        name="sb_attn",
    )(qkv, qkv, qkv)


def _t5_bucket(dist):
    max_exact = N_BUCKETS // 2
    d = jnp.maximum(dist, 1).astype(F32)
    large = max_exact + (jnp.log(d / max_exact) / math.log(MAX_DISTANCE / max_exact)
                         * (N_BUCKETS - max_exact)).astype(jnp.int32)
    large = jnp.minimum(large, N_BUCKETS - 1)
    return jnp.where(dist < max_exact, dist, large)


def _bias_blocks(rel_bias_group, dilation):
    qi = jnp.arange(ATT_BLOCK)[:, None]
    kj = jnp.arange(2 * ATT_BLOCK)[None, :] - ATT_BLOCK
    dist = qi - kj
    bucket = _t5_bucket(jnp.maximum(dist, 0) * dilation)
    onehot = (bucket[None] == jnp.arange(N_BUCKETS)[:, None, None]).astype(F32)
    bias = jnp.einsum('nh,nqk->hqk', rel_bias_group.astype(F32), onehot,
                      precision=lax.Precision.HIGHEST)
    return bias.reshape(2, 2, ATT_BLOCK, 2 * ATT_BLOCK)


DIL_ROWS = 512


def _dil_kernel(q_ref, kp_ref, kc_ref, vp_ref, vc_ref, bias_ref, o_ref, lse_ref):
    n = pl.program_id(2)
    blk = ATT_BLOCK
    nb = q_ref.shape[0] // blk
    pairs = q_ref.shape[1] // LANES
    lane = lax.broadcasted_iota(jnp.int32, (1, LANES), 1)
    head0 = lane < HEAD_DIM
    row = lax.broadcasted_iota(jnp.int32, (blk, 2 * blk), 0)
    col = lax.broadcasted_iota(jnp.int32, (blk, 2 * blk), 1)
    dist = row - col + blk
    window = (dist >= 0) & (dist <= blk)
    first = window & ((n > 0) | (col >= blk))
    keys = jnp.concatenate([kp_ref[...], kc_ref[...]], axis=0)
    vals = jnp.concatenate([vp_ref[...], vc_ref[...]], axis=0)
    q = q_ref[...]
    zero = jnp.zeros_like(q)
    qh = (jnp.where(jnp.tile(head0, (1, pairs)), q, zero), jnp.where(jnp.tile(head0, (1, pairs)), zero, q))
    chains = [(i, p, h) for i in range(nb) for p in range(pairs) for h in range(2)]
    span = lambda i: slice(i * blk, (i + 2) * blk)
    lanes = lambda p: slice(p * LANES, (p + 1) * LANES)
    zs = [_dot_nt(qh[h][i * blk:(i + 1) * blk, lanes(p)], keys[span(i), lanes(p)])
          for i, p, h in chains]
    logits = [jnp.where(first if i == 0 else window, z + bias_ref[p, h], NEG_INF)
              for (i, p, h), z in zip(chains, zs)]
    ms = [jnp.max(l, axis=1, keepdims=True) for l in logits]
    ps = [jnp.exp(l - m) for l, m in zip(logits, ms)]
    dens = [jnp.sum(p_, axis=1, keepdims=True) for p_ in ps]
    outs = [_dot((p_ / den).astype(BF16), vals[span(i), lanes(p)])
            for (i, p, _), p_, den in zip(chains, ps, dens)]
    lses = [m + jnp.log(den) for m, den in zip(ms, dens)]
    for c in range(0, len(chains), 2):
        i, p, _ = chains[c]
        rows = slice(i * blk, (i + 1) * blk)
        o_ref[rows, lanes(p)] = jnp.where(head0, outs[c], outs[c + 1])
        lse_ref[rows, lanes(p)] = jnp.where(head0, lses[c], lses[c + 1])


def _dil_attention(src, col_blocks, bias):
    b, r, l, _ = src.shape
    width = DIL_GROUP_HEADS * HEAD_DIM
    blk = ATT_BLOCK
    rows = min(DIL_ROWS, l)
    nb = rows // blk
    assert all(off * LANES % width == 0 for off in col_blocks)
    qo, ko, vo = (off * LANES // width for off in col_blocks)
    cur = lambda off: (lambda bi, c, n: (bi, c, n, off))
    prev = lambda off: (lambda bi, c, n: (bi, c, jnp.maximum(n * nb - 1, 0), off))
    tile = (None, None, rows, width)
    one = (None, None, blk, width)
    out = jax.ShapeDtypeStruct((b, r, l, width), F32)
    return pl.pallas_call(
        _dil_kernel,
        out_shape=(out, out),
        grid=(b, r, l // rows),
        in_specs=[
            pl.BlockSpec(tile, cur(qo)),
            pl.BlockSpec(one, prev(ko)),
            pl.BlockSpec(tile, cur(ko)),
            pl.BlockSpec(one, prev(vo)),
            pl.BlockSpec(tile, cur(vo)),
            _resident(bias.shape),
        ],
        out_specs=(pl.BlockSpec(tile, cur(0)), pl.BlockSpec(tile, cur(0))),
        compiler_params=_params("parallel", "parallel", "arbitrary"),
        name="dil_attn",
    )(src, src, src, src, src, bias)


OUT_TM = 512


def _attn_out_kernel(x_ref, sb_ref, o0_ref, o1_ref, o2_ref, l0_ref, l1_ref, l2_ref, w_ref,
                     g_ref, wg_ref, wu_ref, wd_ref, y_ref, *order_refs):
    tm = x_ref.shape[0]
    scratch = list(order_refs)

    def natural(ref):
        r = ref.shape[0]
        if r == 1:
            return ref[0]
        buf = scratch.pop()
        for c in range(r):
            for j in range(buf.shape[0]):
                buf[j, pl.ds(c, tm // r, stride=r), :] = ref[c, :, j * LANES:(j + 1) * LANES]
        return jnp.concatenate([buf[j] for j in range(buf.shape[0])], axis=1)

    l0, l1, l2 = natural(l0_ref), natural(l1_ref), natural(l2_ref)
    mx = jnp.maximum(jnp.maximum(l0, l1), l2)
    e0, e1, e2 = jnp.exp(l0 - mx), jnp.exp(l1 - mx), jnp.exp(l2 - mx)
    den = e0 + e1 + e2
    out_b = ((e0 / den) * natural(o0_ref) + (e1 / den) * natural(o1_ref)
             + (e2 / den) * natural(o2_ref))
    na = sb_ref.shape[1]
    y = _dot(sb_ref[...], w_ref[:na, :]) + _dot(out_b.astype(BF16), w_ref[na:, :])
    y_ref[...] = _ffn_apply(x_ref[...] + y, g_ref, wg_ref, wu_ref, wd_ref)


def _attn_out(x3, sb, outs, lses, w_out, ffn):
    b, s, d = x3.shape
    tm = min(OUT_TM, s)
    wide = sb.shape[2]
    row = lambda w: pl.BlockSpec((None, tm, w), lambda bi, t: (bi, t, 0))
    stream = lambda a: pl.BlockSpec((None, a.shape[1], tm // a.shape[1], wide),
                                    lambda bi, t: (bi, 0, t, 0))
    n_buffers = 2 * sum(1 for a in outs if a.shape[1] > 1)
    return pl.pallas_call(
        _attn_out_kernel,
        out_shape=jax.ShapeDtypeStruct((b, s, d), F32),
        grid=(b, s // tm),
        in_specs=[row(d), row(wide)] + [stream(a) for a in (*outs, *lses)]
        + [_resident(w_out.shape)] + _ffn_specs(ffn),
        out_specs=row(d),
        scratch_shapes=[pltpu.VMEM((wide // LANES, tm, LANES), F32)] * n_buffers,
        compiler_params=_params("parallel", "parallel"),
        name="attn_out_ffn",
    )(x3, sb, *outs, *lses, w_out, *ffn)


RW_TS = 512


def _rwkv_in_kernel(x_ref, xp_ref, g_ref, mix_ref, wr_ref, wk_ref, wv_ref,
                    w1_ref, w2_ref, a1_ref, a2_ref, g1_ref, g2_ref,
                    r_ref, k_ref, v_ref, lw_ref, la_ref, gate_ref):
    si = pl.program_id(1)
    gain = g_ref[...]
    h = _rms(x_ref[...], gain)
    prev_last = _rms(xp_ref[...], gain)[SUBLANES - 1:, :] * (si > 0).astype(F32)
    rows = lax.broadcasted_iota(jnp.int32, h.shape, 0)
    shifted = jnp.where(rows == 0, prev_last, pltpu.roll(h, 1, axis=0))
    xx = shifted - h
    mixed = lambda i: (h + xx * mix_ref[i:i + 1, :]).astype(BF16)

    r_ref[...] = _dot(mixed(0), wr_ref[...]).astype(r_ref.dtype)
    k_ref[...] = _dot(mixed(2), wk_ref[...]).astype(k_ref.dtype)
    v_ref[...] = _dot(mixed(3), wv_ref[...]).astype(v_ref.dtype)
    lw_ref[...] = _dot(jnp.tanh(_dot(mixed(1), w1_ref[...])).astype(BF16), w2_ref[...])
    la_ref[...] = _dot(_dot(mixed(4), a1_ref[...]).astype(BF16), a2_ref[...]).astype(la_ref.dtype)
    gate_ref[...] = _dot(jax.nn.sigmoid(_dot(mixed(5), g1_ref[...])).astype(BF16),
                         g2_ref[...]).astype(gate_ref.dtype)


def _rwkv_in(x3, g, mix, wr, wk, wv, w1, w2, a1, a2, g1, g2):
    b, s, d = x3.shape
    ts = min(RW_TS, s)
    tile = pl.BlockSpec((None, ts, d), lambda bi, si: (bi, si, 0))
    prev = pl.BlockSpec((None, SUBLANES, d),
                        lambda bi, si: (bi, jnp.maximum(si * (ts // SUBLANES) - 1, 0), 0))
    out = lambda dt: jax.ShapeDtypeStruct((b, s, d), dt)
    weights = (wr, wk, wv, w1, w2, a1, a2, g1, g2)
    return pl.pallas_call(
        _rwkv_in_kernel,
        out_shape=(out(BF16), out(BF16), out(BF16), out(F32), out(BF16), out(BF16)),
        grid=(b, s // ts),
        in_specs=[tile, prev, _resident((1, d)), _resident(mix.shape)]
        + [_resident(w.shape) for w in weights],
        out_specs=(tile,) * 6,
        compiler_params=_params("parallel", "arbitrary"),
        name="rwkv_in",
    )(x3, x3, g.reshape(1, d), mix, *weights)


SCAN_ROWS = 1024
SCAN_GROUPS = 2
SCAN_SKEW = 4


def _scan_kernel(tiles_per_seq, r_ref, k_ref, v_ref, lw_ref, la_ref, vec_ref, y_ref, bonus_ref,
                 state_ref, qm_ref, y3_ref, n_ref):
    c_len = SCAN_CHUNK
    two = 2 * c_len
    n_chunks = r_ref.shape[0] // c_len
    step = pl.program_id(0)

    @pl.when(step == 0)
    def _():
        state_ref[...] = jnp.zeros_like(state_ref)
        qm_ref[...] = jnp.zeros_like(qm_ref)
        y3_ref[...] = jnp.zeros_like(y3_ref)
        n_ref[...] = jnp.zeros_like(n_ref)

    lane = lax.broadcasted_iota(jnp.int32, (1, LANES), 1)
    head0 = lane < HEAD_DIM
    ti = lax.broadcasted_iota(jnp.int32, (c_len, two), 0)
    ii = lax.broadcasted_iota(jnp.int32, (c_len, two), 1) % c_len
    strict, incl, ident = ti > ii, ti >= ii, ti == ii
    ri = lax.broadcasted_iota(jnp.int32, (LANES, LANES), 0)
    ci = lax.broadcasted_iota(jnp.int32, (LANES, LANES), 1)
    same_head = (ri // HEAD_DIM) == (ci // HEAD_DIM)
    eye = ri == ci
    tri = (lax.broadcasted_iota(jnp.int32, (c_len, c_len), 0)
           >= lax.broadcasted_iota(jnp.int32, (c_len, c_len), 1)).astype(BF16)
    tri2 = jnp.concatenate([tri, tri], axis=1)

    def stack(x):
        zero = jnp.zeros_like(x)
        return jnp.concatenate([jnp.where(head0, x, zero), jnp.where(head0, zero, x)], axis=0)

    bf = lambda x: x.astype(BF16)
    each = lambda f, *ls: [f(*xs) for xs in zip(*ls)]

    w0, a0, k_k, k_a, r_k = (vec_ref[i:i + 1, :] for i in range(5))
    ones = _head_ones()

    def token_maps(rw):
        r, k, v, la = (ref[rw, :].astype(F32) for ref in (r_ref, k_ref, v_ref, la_ref))
        t = -(w0 + lw_ref[rw, :])
        w_log = -(jnp.maximum(t, 0.0) + jnp.log(1.0 + jnp.exp(-jnp.abs(t)))) - 0.5
        a = jax.nn.sigmoid(a0 + la)
        return dict(r=r, v=v, a=a, kk=k * k_k, k=k * (1.0 + (a - 1.0) * k_a),
                    ld=-jnp.exp(w_log))

    def prepare(x, norm, cum):
        r, k, v, ld = x["r"], x["k"], x["v"], x["ld"]
        kk = x["kk"] / jnp.maximum(norm, 1e-12)
        na, bb = -kk, kk * x["a"]
        total = cum[c_len - 1:, :]
        e_neg = jnp.exp(-cum)
        e_tail = jnp.exp(total - cum)
        a_t = bf(na * jnp.exp(cum - ld))
        r_t = r * jnp.exp(cum)
        vb = bf(v)
        tails = jnp.concatenate([bb * e_tail, k * e_tail], axis=0)
        return dict(
            r_t=r_t, a_st=stack(a_t), vb=vb, v_st=stack(vb),
            lhs=jnp.concatenate([a_t, bf(r_t)], axis=0),
            rhs=jnp.concatenate([stack(bf(bb * e_neg)), stack(bf(k * e_neg))], axis=0),
            tails_t=bf(jnp.transpose(tails)),
            decay=jnp.where(eye, jnp.exp(total), 0.0))

    def transitions(rows, out):
        n = len(rows)
        xs = []
        for g, rw in enumerate(rows):
            xs.append(token_maps(rw))
            if g % 2:
                yield
        parts = [_split_bf16(x["ld"]) for x in xs]
        his = jnp.concatenate([p[0] for p in parts], axis=1) if n > 1 else parts[0][0]
        los = jnp.concatenate([p[1] for p in parts], axis=1) if n > 1 else parts[0][1]
        cum_all = _dot(tri2, jnp.concatenate([his, los], axis=0))
        cums = [cum_all[:, g * LANES:(g + 1) * LANES] for g in range(n)]
        by_chunk = lambda a: [a[g * c_len:(g + 1) * c_len] for g in range(n)]
        norms = by_chunk(jnp.sqrt(_dot(bf(jnp.concatenate([x["kk"] * x["kk"] for x in xs], axis=0)),
                                       ones)))
        hi, lo = _split_bf16(jnp.concatenate([x["r"] * x["k"] * r_k for x in xs], axis=0))
        for rw, x, wgt in zip(rows, xs, by_chunk(_dot(hi, ones) + _dot(lo, ones))):
            bonus_ref[rw, :] = wgt * x["v"]
        ps = each(prepare, xs, norms, cums)
        yield

        grams = [_dot_nt(p["lhs"], p["rhs"]) for p in ps]
        n_abs = [jnp.where(strict, g[:c_len, :two], 0.0) for g in grams]
        t1s = [_dot(bf(jnp.where(strict, g[:c_len, two:], 0.0)), p["v_st"])
               for g, p in zip(grams, ps)]
        b_rs = [bf(jnp.concatenate([jnp.where(incl, g[c_len:, :two], 0.0),
                                    jnp.where(incl, g[c_len:, two:], 0.0)], axis=1)) for g in grams]
        yield

        pws = [_dot(bf(x), stack(bf(x))) for x in n_abs]
        invs = [jnp.where(ident, 1.0, x) for x in n_abs]
        yield
        terms = 2
        while terms < c_len:
            last = 2 * terms >= c_len
            nxt_p, nxt_t = [], []
            for pw, inv in zip(pws, invs):
                pwb = bf(pw)
                if last:
                    nxt_t.append(inv + _dot(pwb, stack(bf(inv))))
                else:
                    both = _dot(pwb, jnp.concatenate([stack(pwb), stack(bf(inv))], axis=1))
                    nxt_p.append(both[:, :two])
                    nxt_t.append(inv + both[:, two:])
            pws, invs = nxt_p, nxt_t
            terms *= 2
            yield

        wus = [bf(_dot(bf(inv), jnp.concatenate([p["a_st"], stack(bf(t1))], axis=1)))
               for inv, p, t1 in zip(invs, ps, t1s)]
        yield
        zeros = jnp.zeros((c_len, LANES), BF16)
        tops = [_dot(p["tails_t"], jnp.concatenate(
                    [wu, jnp.concatenate([zeros, p["vb"]], axis=1)], axis=0))
                for p, wu in zip(ps, wus)]
        lows = [_dot(b_r, jnp.concatenate(
                    [jnp.concatenate([stack(wu[:, :LANES]), stack(wu[:, LANES:])], axis=1),
                     jnp.concatenate([jnp.zeros((two, LANES), BF16), p["v_st"]], axis=1)], axis=0))
                for b_r, p, wu in zip(b_rs, ps, wus)]
        for p, top, low in zip(ps, tops, lows):
            m_mat = p["decay"] + jnp.where(same_head, top[:, :LANES], 0.0)
            n_mat = jnp.where(same_head, top[:, LANES:], 0.0)
            q2 = p["r_t"] + low[:, :LANES]
            out.append((bf(jnp.concatenate([q2, m_mat], axis=0)), low[:, LANES:], n_mat))

    rows = [pl.ds(g * c_len, c_len) for g in range(n_chunks)]
    starts_seq = (step - 1) % tiles_per_seq == 0
    chain = {"state": jnp.where(starts_seq, 0.0, state_ref[...]), "next": 0}

    def chain_step():
        g = chain["next"]
        if g == n_chunks:
            return
        chain["next"] = g + 1
        res = _dot(qm_ref[g], bf(chain["state"]))
        y_ref[rows[g], :] = res[:c_len] + y3_ref[g]
        chain["state"] = res[c_len:] + n_ref[g]

    size = n_chunks // SCAN_GROUPS
    results = [[] for _ in range(SCAN_GROUPS)]
    waiting = [transitions(rows[i * size:(i + 1) * size], results[i]) for i in range(SCAN_GROUPS)]
    running, slot = [], 0
    while waiting or running:
        if waiting and slot % SCAN_SKEW == 0:
            running.append(waiting.pop(0))
        for gen in list(running):
            if next(gen, "done") == "done":
                running.remove(gen)
        chain_step()
        slot += 1
    while chain["next"] < n_chunks:
        chain_step()
    state_ref[...] = chain["state"]
    for g, (qm, y3, n_mat) in enumerate(sum(results, [])):
        qm_ref[g] = qm
        y3_ref[g] = y3
        n_ref[g] = n_mat


def _scan(r, k, v, lw, la, vecs):
    b, s, d = r.shape
    rows = min(SCAN_ROWS, s)
    assert s % rows == 0 and rows % SCAN_CHUNK == 0
    pairs, tiles = d // LANES, s // rows
    n_tiles = b * pairs * tiles
    n_chunks = rows // SCAN_CHUNK

    def tile_of(j):
        return j // (pairs * tiles), j % tiles, (j // tiles) % pairs

    in_tile = pl.BlockSpec((None, rows, LANES), lambda i: tile_of(jnp.minimum(i, n_tiles - 1)))
    out_tile = pl.BlockSpec((None, rows, LANES), lambda i: tile_of(jnp.maximum(i - 1, 0)))
    vec_tile = pl.BlockSpec((vecs.shape[0], LANES),
                            lambda i: (0, tile_of(jnp.minimum(i, n_tiles - 1))[2]))
    out = jax.ShapeDtypeStruct((b, s, d), F32)
    return pl.pallas_call(
        functools.partial(_scan_kernel, tiles),
        out_shape=(out, out),
        grid=(n_tiles + 1,),
        in_specs=[in_tile] * 5 + [vec_tile],
        out_specs=(out_tile, in_tile),
        scratch_shapes=[pltpu.VMEM((LANES, LANES), F32),
                        pltpu.VMEM((n_chunks, SCAN_CHUNK + LANES, LANES), BF16),
                        pltpu.VMEM((n_chunks, SCAN_CHUNK, LANES), F32),
                        pltpu.VMEM((n_chunks, LANES, LANES), F32)],
        compiler_params=_params("arbitrary"),
        name="rwkv_scan",
    )(r, k, v, lw, la, vecs)


RW_OUT_TM = 512


def _rwkv_out_kernel(x_ref, y_ref, bonus_ref, gate_ref, lg_ref, lb_ref, wo_ref,
                     g_ref, wg_ref, wu_ref, wd_ref, o_ref):
    y = y_ref[...]
    ones = _head_ones()
    mu = _head_sum(y, ones, exact=True) * (1.0 / HEAD_DIM)
    dlt = y - mu
    var = _head_sum(dlt * dlt, ones, exact=True) * (1.0 / HEAD_DIM)
    yn = dlt * lax.rsqrt(var + GN_EPS) * lg_ref[...] + lb_ref[...] + bonus_ref[...]
    x = x_ref[...] + _dot((yn * gate_ref[...].astype(F32)).astype(BF16), wo_ref[...])
    o_ref[...] = _ffn_apply(x, g_ref, wg_ref, wu_ref, wd_ref)


def _rwkv_out(x2, y2, bonus2, gate2, lnx_g, lnx_b, wo, ffn):
    m, d = x2.shape
    tm = min(RW_OUT_TM, m)
    row = pl.BlockSpec((tm, d), lambda i: (i, 0))
    return pl.pallas_call(
        _rwkv_out_kernel,
        out_shape=jax.ShapeDtypeStruct((m, d), F32),
        grid=(m // tm,),
        in_specs=[row, row, row, row, _resident((1, d)), _resident((1, d)), _resident(wo.shape)]
        + _ffn_specs(ffn),
        out_specs=row,
        compiler_params=_params("parallel"),
        name="rwkv_out_ffn",
    )(x2, y2, bonus2, gate2, lnx_g.reshape(1, d), lnx_b.reshape(1, d), wo, *ffn)


def _attention_layer(x3, mix_g, rel_bias, w_in, q_norm, k_norm, w_out, ffn):
    b, s, d = x3.shape
    nat, streams = _proj(x3, mix_g, w_in, q_norm, k_norm)
    sb = _sb_attention(nat)
    outs, lses = [], []
    group_w = DIL_GROUP_HEADS * HEAD_DIM
    for g, r in enumerate(DILATIONS):
        if r == 1:
            src = nat.reshape(b, 1, s, -1)
            base = SB_COLS + GROUP_COLS * g
        else:
            src, base = streams.pop(0), 0
        offs = tuple((base + part * group_w) // LANES for part in range(3))
        bias = _bias_blocks(rel_bias[:, g * DIL_GROUP_HEADS:(g + 1) * DIL_GROUP_HEADS], r)
        o, l = _dil_attention(src, offs, bias)
        outs.append(o)
        lses.append(l)
    return _attn_out(x3, sb, outs, lses, w_out.astype(BF16), ffn)


def _pad_cols(w, n):
    return jnp.pad(w, ((0, 0), (0, n - w.shape[1])))


def _pad_rows(w, n):
    return jnp.pad(w, ((0, n - w.shape[0]), (0, 0)))


def _rwkv_layer(x3, mix_g, mix, w0, w1, w2, a0, a1, a2, g1, g2, k_k, k_a, r_k,
                w_r, w_k, w_v, w_o, lnx_g, lnx_b, ffn):
    b, s, d = x3.shape
    bf = lambda w: w.astype(BF16)
    lora = lambda w: -(-w // LANES) * LANES
    vecs = jnp.stack([w0, a0, k_k, k_a, r_k.reshape(d), jnp.zeros_like(w0),
                      jnp.zeros_like(w0), jnp.zeros_like(w0)], axis=0).astype(F32)
    dw, da, dg = lora(w1.shape[1]), lora(a1.shape[1]), lora(g1.shape[1])
    r, k, v, lw, la, gate = _rwkv_in(
        x3, mix_g, mix, bf(w_r), bf(w_k), bf(w_v),
        bf(_pad_cols(w1, dw)), bf(_pad_rows(w2, dw)),
        bf(_pad_cols(a1, da)), bf(_pad_rows(a2, da)),
        bf(_pad_cols(g1, dg)), bf(_pad_rows(g2, dg)))
    y, bonus = _scan(r, k, v, lw, la, vecs)
    flat = lambda t: t.reshape(b * s, d)
    out = _rwkv_out(flat(x3), flat(y), flat(bonus), flat(gate), lnx_g, lnx_b, bf(w_o), ffn)
    return out.reshape(b, s, d)


def kernel(x, ffn_norm, ffn_w_gate, ffn_w_up, ffn_w_down, mix_norm, rel_bias, attn_w_in, attn_q_norm, attn_k_norm, attn_w_out, rw_mix, rw_w0, rw_w1, rw_w2, rw_a0, rw_a1, rw_a2, rw_g1, rw_g2, rw_kk, rw_ka, rw_rk, rw_wr, rw_wk, rw_wv, rw_wo, rw_lnx_g, rw_lnx_b):
    b, s, d = x.shape
    depth = ffn_norm.shape[0]

    def ffn_operands(layer, half):
        return (ffn_norm[layer, half].reshape(1, d).astype(F32),
                ffn_w_gate[layer, half].astype(BF16), ffn_w_up[layer, half].astype(BF16),
                ffn_w_down[layer, half].astype(BF16))

    for layer in range(depth):
        x = _ffn(x.reshape(b * s, d), *ffn_operands(layer, 0)).reshape(b, s, d)
        second = ffn_operands(layer, 1)
        if layer % 2 == 0:
            e = layer // 2
            x = _attention_layer(x, mix_norm[layer], rel_bias, attn_w_in[e], attn_q_norm[e],
                                 attn_k_norm[e], attn_w_out[e], second)
        else:
            o = layer // 2
            x = _rwkv_layer(x, mix_norm[layer], rw_mix[o], rw_w0[o], rw_w1[o], rw_w2[o],
                            rw_a0[o], rw_a1[o], rw_a2[o], rw_g1[o], rw_g2[o], rw_kk[o],
                            rw_ka[o], rw_rk[o], rw_wr[o], rw_wk[o], rw_wv[o], rw_wo[o],
                            rw_lnx_g[o], rw_lnx_b[o], second)
    return x
```

```python
import functools
import math

import jax
import jax.numpy as jnp
from jax import lax
from jax.experimental import pallas as pl
from jax.experimental.pallas import tpu as pltpu

F32 = jnp.float32
BF16 = jnp.bfloat16

HEAD_DIM = 64
LANES = 128
SUBLANES = 8
SB_HEADS = 4
DIL_HEADS = 12
DIL_PATTERNS = ((128, 1), (512, 4), (2048, 16))
DIL_GROUP_HEADS = 4
ATT_BLOCK = 128
N_BUCKETS = 32
MAX_DISTANCE = 2048
NORM_EPS = 1e-6
GN_EPS = 64e-5
NEG_INF = -1e30
LOG2E = math.log2(math.e)
SCAN_CHUNK = 64
VMEM_LIMIT = 56 * 1024 * 1024


def _params(*sem):
    return pltpu.CompilerParams(dimension_semantics=sem, vmem_limit_bytes=VMEM_LIMIT)


def _resident(shape):
    nd = len(shape)
    return pl.BlockSpec(shape, lambda *_: (0,) * nd, pipeline_mode=pl.Buffered(1))


def _rms(x, g):
    ms = jnp.mean(x * x, axis=-1, keepdims=True)
    return x * lax.rsqrt(ms + NORM_EPS) * g


def _dot(a, b):
    return jnp.dot(a, b, preferred_element_type=F32)


def _dot_nt(a, b):
    return lax.dot_general(a, b, (((1,), (1,)), ((), ())), preferred_element_type=F32)


def _split_bf16(x):
    hi = x.astype(BF16)
    lo = (x - hi.astype(F32)).astype(BF16)
    return hi, lo


def _head_ones():
    r = lax.broadcasted_iota(jnp.int32, (LANES, LANES), 0) // HEAD_DIM
    c = lax.broadcasted_iota(jnp.int32, (LANES, LANES), 1) // HEAD_DIM
    return (r == c).astype(BF16)


def _head_sum(x, ones, exact):
    outs = []
    for c in range(x.shape[1] // LANES):
        xc = x[:, c * LANES:(c + 1) * LANES]
        if exact:
            hi, lo = _split_bf16(xc)
            outs.append(_dot(hi, ones) + _dot(lo, ones))
        else:
            outs.append(_dot(xc.astype(BF16), ones))
    return outs[0] if len(outs) == 1 else jnp.concatenate(outs, axis=1)


FFN_TM = 512
FFN_FK = 256


def _ffn_apply(x, g_ref, wg_ref, wu_ref, wd_ref):
    h = _rms(x, g_ref[...]).astype(BF16)
    d_ff = wg_ref.shape[1]
    acc = jnp.zeros(x.shape, F32)
    for c in range(d_ff // FFN_FK):
        sl = slice(c * FFN_FK, (c + 1) * FFN_FK)
        gate = _dot(h, wg_ref[:, sl])
        up = _dot(h, wu_ref[:, sl])
        act = (gate * jax.nn.sigmoid(gate) * up).astype(BF16)
        acc = acc + _dot(act, wd_ref[sl, :])
    return x + 0.5 * acc


def _ffn_specs(ffn):
    return [_resident(a.shape) for a in ffn]


def _ffn_kernel(x_ref, g_ref, wg_ref, wu_ref, wd_ref, o_ref):
    o_ref[...] = _ffn_apply(x_ref[...], g_ref, wg_ref, wu_ref, wd_ref)


def _ffn(x2, *ffn):
    m, d = x2.shape
    tm = min(FFN_TM, m)
    return pl.pallas_call(
        _ffn_kernel,
        out_shape=jax.ShapeDtypeStruct((m, d), F32),
        grid=(m // tm,),
        in_specs=[pl.BlockSpec((tm, d), lambda i: (i, 0))] + _ffn_specs(ffn),
        out_specs=pl.BlockSpec((tm, d), lambda i: (i, 0)),
        compiler_params=_params("parallel"),
        name="ffn",
    )(x2, *ffn)


PROJ_TM = 512
PROJ_NC = 256
SB_COLS = 3 * SB_HEADS * HEAD_DIM
DIL_COLS = DIL_HEADS * HEAD_DIM
GROUP_COLS = 3 * DIL_GROUP_HEADS * HEAD_DIM
DILATIONS = tuple(r for _, r in DIL_PATTERNS)
assert PROJ_NC == DIL_GROUP_HEADS * HEAD_DIM


def _proj_kernel(x_ref, g_ref, w_ref, qn_ref, kn_ref, nat_ref, *rest):
    stream_refs, pt_ref = rest[:-1], rest[-1]
    h = _rms(x_ref[...], g_ref[...]).astype(BF16)
    tm = x_ref.shape[0]
    ones = _head_ones()
    scale = HEAD_DIM ** -0.5
    streams = dict(zip([g for g, r in enumerate(DILATIONS) if r > 1], stream_refs))
    n_chunks = w_ref.shape[1] // PROJ_NC
    cols = lambda c: slice(c * PROJ_NC, (c + 1) * PROJ_NC)
    group_of = lambda c: c // 3 - 1

    def finish(c, p):
        part, grp = c % 3, group_of(c)
        if grp >= 0 and part < 2:
            norm_gain = qn_ref[...] if part == 0 else kn_ref[...]
            ms = _head_sum(p * p, ones, exact=False) * (1.0 / HEAD_DIM)
            p = p * lax.rsqrt(ms + NORM_EPS) * norm_gain
        if part == 0:
            p = p * (scale * LOG2E if grp < 0 else scale)
        if grp not in streams:
            nat_ref[:, cols(c)] = p.astype(BF16)
            return
        r = DILATIONS[grp]
        n = tm // r
        for j in range(PROJ_NC // LANES):
            buf = pt_ref.at[c % 2, j]
            buf[...] = p[:, j * LANES:(j + 1) * LANES]
            lanes = slice(part * PROJ_NC + j * LANES, part * PROJ_NC + (j + 1) * LANES)
            for cc in range(r):
                streams[grp][cc, :, lanes] = buf[pl.ds(cc, n, stride=r), :].astype(BF16)

    product = lambda c: _dot(h, w_ref[:, cols(c)])
    pending = product(0)
    for c in range(1, n_chunks):
        nxt = product(c)
        finish(c - 1, pending)
        pending = nxt
    finish(n_chunks - 1, pending)


def _proj(x3, g, w_in, q_norm, k_norm):
    b, s, d = x3.shape
    tm = min(PROJ_TM, s)
    tiles = s // tm
    assert all(DILATIONS[g] == 1 for g in range(len(DILATIONS)) if g < DILATIONS.count(1))
    group_w = DIL_GROUP_HEADS * HEAD_DIM
    pieces = [w_in[:, :SB_COLS]] + [
        w_in[:, SB_COLS + part * DIL_COLS + g * group_w:][:, :group_w]
        for g in range(len(DILATIONS)) for part in range(3)]
    w = jnp.concatenate(pieces, axis=1).astype(BF16)
    nat_w = SB_COLS + GROUP_COLS * DILATIONS.count(1)
    tile = lambda v: jnp.tile(v.astype(F32), PROJ_NC // HEAD_DIM).reshape(1, PROJ_NC)
    stream_rs = [r for r in DILATIONS if r > 1]
    outs = pl.pallas_call(
        _proj_kernel,
        out_shape=(jax.ShapeDtypeStruct((b, s, nat_w), BF16),)
        + tuple(jax.ShapeDtypeStruct((b, r, s // r, GROUP_COLS), BF16) for r in stream_rs),
        grid=(b, tiles),
        in_specs=[
            pl.BlockSpec((None, tm, d), lambda bi, t: (bi, t, 0)),
            _resident((1, d)),
            _resident(w.shape),
            _resident((1, PROJ_NC)),
            _resident((1, PROJ_NC)),
        ],
        out_specs=(pl.BlockSpec((None, tm, nat_w), lambda bi, t: (bi, t, 0)),)
        + tuple(pl.BlockSpec((None, r, tm // r, GROUP_COLS), lambda bi, t: (bi, 0, t, 0))
                for r in stream_rs),
        scratch_shapes=[pltpu.VMEM((2, PROJ_NC // LANES, tm, LANES), F32)],
        compiler_params=_params("parallel", "parallel"),
        name="attn_proj",
    )(x3, g.reshape(1, d), w, tile(q_norm), tile(k_norm))
    return outs[0], list(outs[1:])


SB_TQ = 512
SB_UNROLL = 4


def _sb_kernel(q_ref, k_ref, v_ref, o_ref, qh_ref, acc_ref, run_ref):
    qi = pl.program_id(2)
    tq = q_ref.shape[0]
    tk = ATT_BLOCK
    nsub = tq // tk
    lane = lax.broadcasted_iota(jnp.int32, (1, LANES), 1)
    head0 = lane < HEAD_DIM
    q = q_ref[...]
    zero = jnp.zeros_like(q)
    qh_ref[0] = jnp.where(head0, q, zero)
    qh_ref[1] = jnp.where(head0, zero, q)
    acc_ref[...] = jnp.zeros_like(acc_ref)
    run_ref[...] = jnp.zeros_like(run_ref)
    later = (lax.broadcasted_iota(jnp.int32, (tk, tk), 0)
             > lax.broadcasted_iota(jnp.int32, (tk, tk), 1)).astype(BF16)

    def step(blocks, masked):
        starts = [pl.multiple_of(j * tk, tk) for j, _ in blocks]
        kbs = [k_ref[pl.ds(st, tk), :] for st in starts]
        vbs = [v_ref[pl.ds(st, tk), :] for st in starts]
        chains = [(b, h) for b in range(len(blocks)) for h in range(2)]
        zs = [_dot_nt(qh_ref[h, blocks[b][1]:, :], kbs[b]) for b, h in chains]
        log_betas = [jnp.minimum(z, 0.0) - jnp.log(1.0 + jnp.exp2(-jnp.abs(z))) * LOG2E for z in zs]
        log_keeps = [lb - z for lb, z in zip(log_betas, zs)]
        if masked:
            stricts = [lax.broadcasted_iota(jnp.int32, z.shape, 1)
                       < lax.broadcasted_iota(jnp.int32, z.shape, 0) for z in zs]
            log_keeps = [jnp.where(s, lk, 0.0) for s, lk in zip(stricts, log_keeps)]
        afters = [_dot(lk.astype(BF16), later) for lk in log_keeps]
        totals = [jnp.sum(lk, axis=1, keepdims=True) for lk in log_keeps]
        same_rows = len({r0 for _, r0 in blocks}) == 1
        for h in range(2):
            ws = []
            for b, (_, r0) in enumerate(blocks):
                c = 2 * b + h
                run = run_ref[h, r0:, :]
                w = jnp.exp2(log_betas[c] + afters[c] + run)
                if masked:
                    w = jnp.where(stricts[c], w, 0.0)
                ws.append(w.astype(BF16))
                run_ref[h, r0:, :] = run + totals[c]
                if not same_rows:
                    acc_ref[h, r0:, :] += _dot(ws[-1], vbs[b])
            if same_rows:
                r0 = blocks[0][1]
                acc_ref[h, r0:, :] += _dot(jnp.concatenate(ws, axis=1), jnp.concatenate(vbs, axis=0))

    step([(qi * nsub + c, c * tk) for c in reversed(range(nsub))], True)

    def body(i, carry):
        j = qi * nsub - 1 - SB_UNROLL * i
        step([(j - u, 0) for u in range(SB_UNROLL)], False)
        return carry

    lax.fori_loop(0, qi * (nsub // SB_UNROLL), body, 0)
    o_ref[...] = jnp.where(head0, acc_ref[0], acc_ref[1]).astype(o_ref.dtype)


def _sb_attention(qkv):
    b, s, _ = qkv.shape
    pairs = SB_HEADS * HEAD_DIM // LANES
    tq = min(SB_TQ, s)
    return pl.pallas_call(
        _sb_kernel,
        out_shape=jax.ShapeDtypeStruct((b, s, SB_HEADS * HEAD_DIM), BF16),
        grid=(b, pairs, s // tq),
        in_specs=[
            pl.BlockSpec((None, tq, LANES), lambda bi, p, i: (bi, i, p)),
            pl.BlockSpec((None, s, LANES), lambda bi, p, i: (bi, 0, pairs + p)),
            pl.BlockSpec((None, s, LANES), lambda bi, p, i: (bi, 0, 2 * pairs + p)),
        ],
        out_specs=pl.BlockSpec((None, tq, LANES), lambda bi, p, i: (bi, i, p)),
        scratch_shapes=[pltpu.VMEM((2, tq, LANES), BF16), pltpu.VMEM((2, tq, LANES), F32),
                        pltpu.VMEM((2, tq, LANES), F32)],
        compiler_params=_params("parallel", "parallel", "arbitrary"),
        name="sb_attn",
    )(qkv, qkv, qkv)


def _t5_bucket(dist):
    max_exact = N_BUCKETS // 2
    d = jnp.maximum(dist, 1).astype(F32)
    large = max_exact + (jnp.log(d / max_exact) / math.log(MAX_DISTANCE / max_exact)
                         * (N_BUCKETS - max_exact)).astype(jnp.int32)
    large = jnp.minimum(large, N_BUCKETS - 1)
    return jnp.where(dist < max_exact, dist, large)


def _bias_blocks(rel_bias_group, dilation):
    qi = jnp.arange(ATT_BLOCK)[:, None]
    kj = jnp.arange(2 * ATT_BLOCK)[None, :] - ATT_BLOCK
    dist = qi - kj
    bucket = _t5_bucket(jnp.maximum(dist, 0) * dilation)
    onehot = (bucket[None] == jnp.arange(N_BUCKETS)[:, None, None]).astype(F32)
    bias = jnp.einsum('nh,nqk->hqk', rel_bias_group.astype(F32), onehot,
                      precision=lax.Precision.HIGHEST)
    return bias.reshape(2, 2, ATT_BLOCK, 2 * ATT_BLOCK)


DIL_ROWS = 512


def _dil_kernel(q_ref, kp_ref, kc_ref, vp_ref, vc_ref, bias_ref, o_ref, lse_ref):
    n = pl.program_id(2)
    blk = ATT_BLOCK
    nb = q_ref.shape[0] // blk
    pairs = q_ref.shape[1] // LANES
    lane = lax.broadcasted_iota(jnp.int32, (1, LANES), 1)
    head0 = lane < HEAD_DIM
    row = lax.broadcasted_iota(jnp.int32, (blk, 2 * blk), 0)
    col = lax.broadcasted_iota(jnp.int32, (blk, 2 * blk), 1)
    dist = row - col + blk
    window = (dist >= 0) & (dist <= blk)
    first = window & ((n > 0) | (col >= blk))
    keys = jnp.concatenate([kp_ref[...], kc_ref[...]], axis=0)
    vals = jnp.concatenate([vp_ref[...], vc_ref[...]], axis=0)
    q = q_ref[...]
    zero = jnp.zeros_like(q)
    first_heads = jnp.tile(head0, (1, pairs))
    qh = (jnp.where(first_heads, q, zero), jnp.where(first_heads, zero, q))
    chains = [(i, p, h) for i in range(nb) for p in range(pairs) for h in range(2)]
    span = lambda i: slice(i * blk, (i + 2) * blk)
    lanes = lambda p: slice(p * LANES, (p + 1) * LANES)
    zs = [_dot_nt(qh[h][i * blk:(i + 1) * blk, lanes(p)], keys[span(i), lanes(p)])
          for i, p, h in chains]
    logits = [jnp.where(first if i == 0 else window, z + bias_ref[p, h], NEG_INF)
              for (i, p, h), z in zip(chains, zs)]
    ms = [jnp.max(l, axis=1, keepdims=True) for l in logits]
    ps = [jnp.exp(l - m) for l, m in zip(logits, ms)]
    dens = [jnp.sum(p_, axis=1, keepdims=True) for p_ in ps]
    outs = [_dot((p_ / den).astype(BF16), vals[span(i), lanes(p)])
            for (i, p, _), p_, den in zip(chains, ps, dens)]
    lses = [m + jnp.log(den) for m, den in zip(ms, dens)]
    for c in range(0, len(chains), 2):
        i, p, _ = chains[c]
        rows = slice(i * blk, (i + 1) * blk)
        o_ref[rows, lanes(p)] = jnp.where(head0, outs[c], outs[c + 1])
        lse_ref[rows, lanes(p)] = jnp.where(head0, lses[c], lses[c + 1])


def _dil_attention(src, col_blocks, bias):
    b, r, l, _ = src.shape
    width = DIL_GROUP_HEADS * HEAD_DIM
    blk = ATT_BLOCK
    rows = min(DIL_ROWS, l)
    nb = rows // blk
    assert all(off * LANES % width == 0 for off in col_blocks)
    qo, ko, vo = (off * LANES // width for off in col_blocks)
    cur = lambda off: (lambda bi, c, n: (bi, c, n, off))
    prev = lambda off: (lambda bi, c, n: (bi, c, jnp.maximum(n * nb - 1, 0), off))
    tile = (None, None, rows, width)
    one = (None, None, blk, width)
    out = jax.ShapeDtypeStruct((b, r, l, width), F32)
    return pl.pallas_call(
        _dil_kernel,
        out_shape=(out, out),
        grid=(b, r, l // rows),
        in_specs=[
            pl.BlockSpec(tile, cur(qo)),
            pl.BlockSpec(one, prev(ko)),
            pl.BlockSpec(tile, cur(ko)),
            pl.BlockSpec(one, prev(vo)),
            pl.BlockSpec(tile, cur(vo)),
            _resident(bias.shape),
        ],
        out_specs=(pl.BlockSpec(tile, cur(0)), pl.BlockSpec(tile, cur(0))),
        compiler_params=_params("parallel", "parallel", "arbitrary"),
        name="dil_attn",
    )(src, src, src, src, src, bias)


OUT_TM = 512


def _attn_out_kernel(x_ref, sb_ref, o0_ref, o1_ref, o2_ref, l0_ref, l1_ref, l2_ref, w_ref,
                     g_ref, wg_ref, wu_ref, wd_ref, y_ref, *order_refs):
    tm = x_ref.shape[0]
    scratch = list(order_refs)

    def natural(ref):
        r = ref.shape[0]
        if r == 1:
            return ref[0]
        buf = scratch.pop()
        for c in range(r):
            for j in range(buf.shape[0]):
                buf[j, pl.ds(c, tm // r, stride=r), :] = ref[c, :, j * LANES:(j + 1) * LANES]
        return jnp.concatenate([buf[j] for j in range(buf.shape[0])], axis=1)

    l0, l1, l2 = natural(l0_ref), natural(l1_ref), natural(l2_ref)
    mx = jnp.maximum(jnp.maximum(l0, l1), l2)
    e0, e1, e2 = jnp.exp(l0 - mx), jnp.exp(l1 - mx), jnp.exp(l2 - mx)
    den = e0 + e1 + e2
    out_b = ((e0 / den) * natural(o0_ref) + (e1 / den) * natural(o1_ref)
             + (e2 / den) * natural(o2_ref))
    na = sb_ref.shape[1]
    y = _dot(sb_ref[...], w_ref[:na, :]) + _dot(out_b.astype(BF16), w_ref[na:, :])
    y_ref[...] = _ffn_apply(x_ref[...] + y, g_ref, wg_ref, wu_ref, wd_ref)


def _attn_out(x3, sb, outs, lses, w_out, ffn):
    b, s, d = x3.shape
    tm = min(OUT_TM, s)
    wide = sb.shape[2]
    row = lambda w: pl.BlockSpec((None, tm, w), lambda bi, t: (bi, t, 0))
    stream = lambda a: pl.BlockSpec((None, a.shape[1], tm // a.shape[1], wide),
                                    lambda bi, t: (bi, 0, t, 0))
    n_buffers = 2 * sum(1 for a in outs if a.shape[1] > 1)
    return pl.pallas_call(
        _attn_out_kernel,
        out_shape=jax.ShapeDtypeStruct((b, s, d), F32),
        grid=(b, s // tm),
        in_specs=[row(d), row(wide)] + [stream(a) for a in (*outs, *lses)]
        + [_resident(w_out.shape)] + _ffn_specs(ffn),
        out_specs=row(d),
        scratch_shapes=[pltpu.VMEM((wide // LANES, tm, LANES), F32)] * n_buffers,
        compiler_params=_params("parallel", "parallel"),
        name="attn_out_ffn",
    )(x3, sb, *outs, *lses, w_out, *ffn)


RW_TS = 512


def _rwkv_in_kernel(x_ref, xp_ref, g_ref, mix_ref, wr_ref, wk_ref, wv_ref,
                    w1_ref, w2_ref, a1_ref, a2_ref, g1_ref, g2_ref,
                    r_ref, k_ref, v_ref, lw_ref, la_ref, gate_ref):
    si = pl.program_id(1)
    gain = g_ref[...]
    h = _rms(x_ref[...], gain)
    prev_last = _rms(xp_ref[...], gain)[SUBLANES - 1:, :] * (si > 0).astype(F32)
    rows = lax.broadcasted_iota(jnp.int32, h.shape, 0)
    shifted = jnp.where(rows == 0, prev_last, pltpu.roll(h, 1, axis=0))
    xx = shifted - h
    mixed = lambda i: (h + xx * mix_ref[i:i + 1, :]).astype(BF16)

    r_ref[...] = _dot(mixed(0), wr_ref[...]).astype(r_ref.dtype)
    k_ref[...] = _dot(mixed(2), wk_ref[...]).astype(k_ref.dtype)
    v_ref[...] = _dot(mixed(3), wv_ref[...]).astype(v_ref.dtype)
    lw_ref[...] = _dot(jnp.tanh(_dot(mixed(1), w1_ref[...])).astype(BF16), w2_ref[...])
    la_ref[...] = _dot(_dot(mixed(4), a1_ref[...]).astype(BF16), a2_ref[...]).astype(la_ref.dtype)
    gate_ref[...] = _dot(jax.nn.sigmoid(_dot(mixed(5), g1_ref[...])).astype(BF16),
                         g2_ref[...]).astype(gate_ref.dtype)


def _rwkv_in(x3, g, mix, wr, wk, wv, w1, w2, a1, a2, g1, g2):
    b, s, d = x3.shape
    ts = min(RW_TS, s)
    tile = pl.BlockSpec((None, ts, d), lambda bi, si: (bi, si, 0))
    prev = pl.BlockSpec((None, SUBLANES, d),
                        lambda bi, si: (bi, jnp.maximum(si * (ts // SUBLANES) - 1, 0), 0))
    out = lambda dt: jax.ShapeDtypeStruct((b, s, d), dt)
    weights = (wr, wk, wv, w1, w2, a1, a2, g1, g2)
    return pl.pallas_call(
        _rwkv_in_kernel,
        out_shape=(out(BF16), out(BF16), out(BF16), out(F32), out(BF16), out(BF16)),
        grid=(b, s // ts),
        in_specs=[tile, prev, _resident((1, d)), _resident(mix.shape)]
        + [_resident(w.shape) for w in weights],
        out_specs=(tile,) * 6,
        compiler_params=_params("parallel", "arbitrary"),
        name="rwkv_in",
    )(x3, x3, g.reshape(1, d), mix, *weights)


SCAN_ROWS = 1024
SCAN_GROUPS = 2
SCAN_SKEW = 2


def _scan_kernel(tiles_per_seq, r_ref, k_ref, v_ref, lw_ref, la_ref, vec_ref, y_ref, bonus_ref,
                 state_ref, qm_ref, y3_ref, n_ref):
    c_len = SCAN_CHUNK
    two = 2 * c_len
    n_chunks = r_ref.shape[0] // c_len
    step = pl.program_id(0)

    @pl.when(step == 0)
    def _():
        state_ref[...] = jnp.zeros_like(state_ref)
        qm_ref[...] = jnp.zeros_like(qm_ref)
        y3_ref[...] = jnp.zeros_like(y3_ref)
        n_ref[...] = jnp.zeros_like(n_ref)

    lane = lax.broadcasted_iota(jnp.int32, (1, LANES), 1)
    head0 = lane < HEAD_DIM
    ti = lax.broadcasted_iota(jnp.int32, (c_len, two), 0)
    ii = lax.broadcasted_iota(jnp.int32, (c_len, two), 1) % c_len
    strict, incl, ident = ti > ii, ti >= ii, ti == ii
    ri = lax.broadcasted_iota(jnp.int32, (LANES, LANES), 0)
    ci = lax.broadcasted_iota(jnp.int32, (LANES, LANES), 1)
    same_head = (ri // HEAD_DIM) == (ci // HEAD_DIM)
    eye = ri == ci
    tri = (lax.broadcasted_iota(jnp.int32, (c_len, c_len), 0)
           >= lax.broadcasted_iota(jnp.int32, (c_len, c_len), 1)).astype(BF16)
    tri2 = jnp.concatenate([tri, tri], axis=1)

    def stack(x):
        zero = jnp.zeros_like(x)
        return jnp.concatenate([jnp.where(head0, x, zero), jnp.where(head0, zero, x)], axis=0)

    bf = lambda x: x.astype(BF16)
    each = lambda f, *ls: [f(*xs) for xs in zip(*ls)]

    w0, a0, k_k, k_a, r_k = (vec_ref[i:i + 1, :] for i in range(5))
    ones = _head_ones()

    def token_maps(rw):
        r, k, v, la = (ref[rw, :].astype(F32) for ref in (r_ref, k_ref, v_ref, la_ref))
        t = -(w0 + lw_ref[rw, :])
        w_log = -(jnp.maximum(t, 0.0) + jnp.log(1.0 + jnp.exp(-jnp.abs(t)))) - 0.5
        a = jax.nn.sigmoid(a0 + la)
        return dict(r=r, v=v, a=a, kk=k * k_k, k=k * (1.0 + (a - 1.0) * k_a),
                    ld=-jnp.exp(w_log))

    def prepare(x, norm, cum):
        r, k, v, ld = x["r"], x["k"], x["v"], x["ld"]
        kk = x["kk"] / jnp.maximum(norm, 1e-12)
        na, bb = -kk, kk * x["a"]
        total = cum[c_len - 1:, :]
        e_neg = jnp.exp(-cum)
        e_tail = jnp.exp(total - cum)
        a_t = bf(na * jnp.exp(cum - ld))
        r_t = r * jnp.exp(cum)
        vb = bf(v)
        tails = jnp.concatenate([bb * e_tail, k * e_tail], axis=0)
        return dict(
            r_t=r_t, a_st=stack(a_t), vb=vb, v_st=stack(vb),
            lhs=jnp.concatenate([a_t, bf(r_t)], axis=0),
            rhs=jnp.concatenate([stack(bf(bb * e_neg)), stack(bf(k * e_neg))], axis=0),
            tails_t=bf(jnp.transpose(tails)),
            decay=jnp.where(eye, jnp.exp(total), 0.0))

    def transitions(rows, out):
        n = len(rows)
        xs = []
        for g, rw in enumerate(rows):
            xs.append(token_maps(rw))
            if g % 2:
                yield
        parts = [_split_bf16(x["ld"]) for x in xs]
        his = jnp.concatenate([p[0] for p in parts], axis=1) if n > 1 else parts[0][0]
        los = jnp.concatenate([p[1] for p in parts], axis=1) if n > 1 else parts[0][1]
        cum_all = _dot(tri2, jnp.concatenate([his, los], axis=0))
        cums = [cum_all[:, g * LANES:(g + 1) * LANES] for g in range(n)]
        by_chunk = lambda a: [a[g * c_len:(g + 1) * c_len] for g in range(n)]
        norms = by_chunk(jnp.sqrt(_dot(bf(jnp.concatenate([x["kk"] * x["kk"] for x in xs], axis=0)),
                                       ones)))
        hi, lo = _split_bf16(jnp.concatenate([x["r"] * x["k"] * r_k for x in xs], axis=0))
        for rw, x, wgt in zip(rows, xs, by_chunk(_dot(hi, ones) + _dot(lo, ones))):
            bonus_ref[rw, :] = wgt * x["v"]
        ps = each(prepare, xs, norms, cums)
        yield

        grams = [_dot_nt(p["lhs"], p["rhs"]) for p in ps]
        n_abs = [jnp.where(strict, g[:c_len, :two], 0.0) for g in grams]
        t1s = [_dot(bf(jnp.where(strict, g[:c_len, two:], 0.0)), p["v_st"])
               for g, p in zip(grams, ps)]
        b_rs = [bf(jnp.concatenate([jnp.where(incl, g[c_len:, :two], 0.0),
                                    jnp.where(incl, g[c_len:, two:], 0.0)], axis=1)) for g in grams]
        yield

        pws = [_dot(bf(x), stack(bf(x))) for x in n_abs]
        invs = [jnp.where(ident, 1.0, x) for x in n_abs]
        yield
        terms = 2
        while terms < c_len:
            last = 2 * terms >= c_len
            nxt_p, nxt_t = [], []
            for pw, inv in zip(pws, invs):
                pwb = bf(pw)
                if last:
                    nxt_t.append(inv + _dot(pwb, stack(bf(inv))))
                else:
                    both = _dot(pwb, jnp.concatenate([stack(pwb), stack(bf(inv))], axis=1))
                    nxt_p.append(both[:, :two])
                    nxt_t.append(inv + both[:, two:])
            pws, invs = nxt_p, nxt_t
            terms *= 2
            yield

        wus = [bf(_dot(bf(inv), jnp.concatenate([p["a_st"], stack(bf(t1))], axis=1)))
               for inv, p, t1 in zip(invs, ps, t1s)]
        yield
        zeros = jnp.zeros((c_len, LANES), BF16)
        tops = [_dot(p["tails_t"], jnp.concatenate(
                    [wu, jnp.concatenate([zeros, p["vb"]], axis=1)], axis=0))
                for p, wu in zip(ps, wus)]
        lows = [_dot(b_r, jnp.concatenate(
                    [jnp.concatenate([stack(wu[:, :LANES]), stack(wu[:, LANES:])], axis=1),
                     jnp.concatenate([jnp.zeros((two, LANES), BF16), p["v_st"]], axis=1)], axis=0))
                for b_r, p, wu in zip(b_rs, ps, wus)]
        for p, top, low in zip(ps, tops, lows):
            m_mat = p["decay"] + jnp.where(same_head, top[:, :LANES], 0.0)
            n_mat = jnp.where(same_head, top[:, LANES:], 0.0)
            q2 = p["r_t"] + low[:, :LANES]
            out.append((bf(jnp.concatenate([q2, m_mat], axis=0)), low[:, LANES:], n_mat))

    rows = [pl.ds(g * c_len, c_len) for g in range(n_chunks)]
    starts_seq = (step - 1) % tiles_per_seq == 0
    chain = {"state": jnp.where(starts_seq, 0.0, state_ref[...]), "next": 0}

    def chain_step():
        g = chain["next"]
        if g == n_chunks:
            return
        chain["next"] = g + 1
        res = _dot(qm_ref[g], bf(chain["state"]))
        y_ref[rows[g], :] = res[:c_len] + y3_ref[g]
        chain["state"] = res[c_len:] + n_ref[g]

    size = n_chunks // SCAN_GROUPS
    results = [[] for _ in range(SCAN_GROUPS)]
    waiting = [transitions(rows[i * size:(i + 1) * size], results[i]) for i in range(SCAN_GROUPS)]
    running, slot = [], 0
    while waiting or running:
        if waiting and slot % SCAN_SKEW == 0:
            running.append(waiting.pop(0))
        for gen in list(running):
            if next(gen, "done") == "done":
                running.remove(gen)
        chain_step()
        slot += 1
    while chain["next"] < n_chunks:
        chain_step()
    state_ref[...] = chain["state"]
    for g, (qm, y3, n_mat) in enumerate(sum(results, [])):
        qm_ref[g] = qm
        y3_ref[g] = y3
        n_ref[g] = n_mat


def _scan(r, k, v, lw, la, vecs):
    b, s, d = r.shape
    rows = min(SCAN_ROWS, s)
    assert s % rows == 0 and rows % (SCAN_GROUPS * SCAN_CHUNK) == 0
    pairs, tiles = d // LANES, s // rows
    n_tiles = b * pairs * tiles
    n_chunks = rows // SCAN_CHUNK

    def tile_of(j):
        return j // (pairs * tiles), j % tiles, (j // tiles) % pairs

    in_tile = pl.BlockSpec((None, rows, LANES), lambda i: tile_of(jnp.minimum(i, n_tiles - 1)))
    out_tile = pl.BlockSpec((None, rows, LANES), lambda i: tile_of(jnp.maximum(i - 1, 0)))
    vec_tile = pl.BlockSpec((vecs.shape[0], LANES),
                            lambda i: (0, tile_of(jnp.minimum(i, n_tiles - 1))[2]))
    out = jax.ShapeDtypeStruct((b, s, d), F32)
    return pl.pallas_call(
        functools.partial(_scan_kernel, tiles),
        out_shape=(out, out),
        grid=(n_tiles + 1,),
        in_specs=[in_tile] * 5 + [vec_tile],
        out_specs=(out_tile, in_tile),
        scratch_shapes=[pltpu.VMEM((LANES, LANES), F32),
                        pltpu.VMEM((n_chunks, SCAN_CHUNK + LANES, LANES), BF16),
                        pltpu.VMEM((n_chunks, SCAN_CHUNK, LANES), F32),
                        pltpu.VMEM((n_chunks, LANES, LANES), F32)],
        compiler_params=_params("arbitrary"),
        name="rwkv_scan",
    )(r, k, v, lw, la, vecs)


RW_OUT_TM = 512


def _rwkv_out_kernel(x_ref, y_ref, bonus_ref, gate_ref, lg_ref, lb_ref, wo_ref,
                     g_ref, wg_ref, wu_ref, wd_ref, o_ref):
    y = y_ref[...]
    ones = _head_ones()
    mu = _head_sum(y, ones, exact=True) * (1.0 / HEAD_DIM)
    dlt = y - mu
    var = _head_sum(dlt * dlt, ones, exact=False) * (1.0 / HEAD_DIM)
    yn = dlt * lax.rsqrt(var + GN_EPS) * lg_ref[...] + lb_ref[...] + bonus_ref[...]
    x = x_ref[...] + _dot((yn * gate_ref[...].astype(F32)).astype(BF16), wo_ref[...])
    o_ref[...] = _ffn_apply(x, g_ref, wg_ref, wu_ref, wd_ref)


def _rwkv_out(x2, y2, bonus2, gate2, lnx_g, lnx_b, wo, ffn):
    m, d = x2.shape
    tm = min(RW_OUT_TM, m)
    row = pl.BlockSpec((tm, d), lambda i: (i, 0))
    return pl.pallas_call(
        _rwkv_out_kernel,
        out_shape=jax.ShapeDtypeStruct((m, d), F32),
        grid=(m // tm,),
        in_specs=[row, row, row, row, _resident((1, d)), _resident((1, d)), _resident(wo.shape)]
        + _ffn_specs(ffn),
        out_specs=row,
        compiler_params=_params("parallel"),
        name="rwkv_out_ffn",
    )(x2, y2, bonus2, gate2, lnx_g.reshape(1, d), lnx_b.reshape(1, d), wo, *ffn)


def _attention_layer(x3, mix_g, rel_bias, w_in, q_norm, k_norm, w_out, ffn):
    b, s, d = x3.shape
    nat, streams = _proj(x3, mix_g, w_in, q_norm, k_norm)
    sb = _sb_attention(nat)
    outs, lses = [], []
    group_w = DIL_GROUP_HEADS * HEAD_DIM
    for g, r in enumerate(DILATIONS):
        if r == 1:
            src = nat.reshape(b, 1, s, -1)
            base = SB_COLS + GROUP_COLS * g
        else:
            src, base = streams.pop(0), 0
        offs = tuple((base + part * group_w) // LANES for part in range(3))
        bias = _bias_blocks(rel_bias[:, g * DIL_GROUP_HEADS:(g + 1) * DIL_GROUP_HEADS], r)
        o, l = _dil_attention(src, offs, bias)
        outs.append(o)
        lses.append(l)
    return _attn_out(x3, sb, outs, lses, w_out.astype(BF16), ffn)


def _pad_cols(w, n):
    return jnp.pad(w, ((0, 0), (0, n - w.shape[1])))


def _pad_rows(w, n):
    return jnp.pad(w, ((0, n - w.shape[0]), (0, 0)))


def _rwkv_layer(x3, mix_g, mix, w0, w1, w2, a0, a1, a2, g1, g2, k_k, k_a, r_k,
                w_r, w_k, w_v, w_o, lnx_g, lnx_b, ffn):
    b, s, d = x3.shape
    bf = lambda w: w.astype(BF16)
    lora = lambda w: -(-w // LANES) * LANES
    vecs = jnp.stack([w0, a0, k_k, k_a, r_k.reshape(d), jnp.zeros_like(w0),
                      jnp.zeros_like(w0), jnp.zeros_like(w0)], axis=0).astype(F32)
    dw, da, dg = lora(w1.shape[1]), lora(a1.shape[1]), lora(g1.shape[1])
    r, k, v, lw, la, gate = _rwkv_in(
        x3, mix_g, mix, bf(w_r), bf(w_k), bf(w_v),
        bf(_pad_cols(w1, dw)), bf(_pad_rows(w2, dw)),
        bf(_pad_cols(a1, da)), bf(_pad_rows(a2, da)),
        bf(_pad_cols(g1, dg)), bf(_pad_rows(g2, dg)))
    y, bonus = _scan(r, k, v, lw, la, vecs)
    flat = lambda t: t.reshape(b * s, d)
    out = _rwkv_out(flat(x3), flat(y), flat(bonus), flat(gate), lnx_g, lnx_b, bf(w_o), ffn)
    return out.reshape(b, s, d)


def kernel(x, ffn_norm, ffn_w_gate, ffn_w_up, ffn_w_down, mix_norm, rel_bias, attn_w_in, attn_q_norm, attn_k_norm, attn_w_out, rw_mix, rw_w0, rw_w1, rw_w2, rw_a0, rw_a1, rw_a2, rw_g1, rw_g2, rw_kk, rw_ka, rw_rk, rw_wr, rw_wk, rw_wv, rw_wo, rw_lnx_g, rw_lnx_b):
    b, s, d = x.shape
    depth = ffn_norm.shape[0]

    def ffn_operands(layer, half):
        return (ffn_norm[layer, half].reshape(1, d).astype(F32),
                ffn_w_gate[layer, half].astype(BF16), ffn_w_up[layer, half].astype(BF16),
                ffn_w_down[layer, half].astype(BF16))

    for layer in range(depth):
        x = _ffn(x.reshape(b * s, d), *ffn_operands(layer, 0)).reshape(b, s, d)
        second = ffn_operands(layer, 1)
        if layer % 2 == 0:
            e = layer // 2
            x = _attention_layer(x, mix_norm[layer], rel_bias, attn_w_in[e], attn_q_norm[e],
                                 attn_k_norm[e], attn_w_out[e], second)
        else:
            o = layer // 2
            x = _rwkv_layer(x, mix_norm[layer], rw_mix[o], rw_w0[o], rw_w1[o], rw_w2[o],
                            rw_a0[o], rw_a1[o], rw_a2[o], rw_g1[o], rw_g2[o], rw_kk[o],
                            rw_ka[o], rw_rk[o], rw_wr[o], rw_wk[o], rw_wv[o], rw_wo[o],
                            rw_lnx_g[o], rw_lnx_b[o], second)
    return x
```

```python
import functools
import math

import jax
import jax.numpy as jnp
from jax import lax
from jax.experimental import pallas as pl
from jax.experimental.pallas import tpu as pltpu

F32 = jnp.float32
BF16 = jnp.bfloat16

HEAD_DIM = 64
LANES = 128
SUBLANES = 8
SB_HEADS = 4
DIL_HEADS = 12
DIL_PATTERNS = ((128, 1), (512, 4), (2048, 16))
DIL_GROUP_HEADS = 4
ATT_BLOCK = 128
N_BUCKETS = 32
MAX_DISTANCE = 2048
NORM_EPS = 1e-6
GN_EPS = 64e-5
NEG_INF = -1e30
LOG2E = math.log2(math.e)
SCAN_CHUNK = 64
VMEM_LIMIT = 56 * 1024 * 1024


def _params(*sem):
    return pltpu.CompilerParams(dimension_semantics=sem, vmem_limit_bytes=VMEM_LIMIT)


def _resident(shape):
    nd = len(shape)
    return pl.BlockSpec(shape, lambda *_: (0,) * nd, pipeline_mode=pl.Buffered(1))


def _rms(x, g):
    ms = jnp.mean(x * x, axis=-1, keepdims=True)
    return x * lax.rsqrt(ms + NORM_EPS) * g


def _dot(a, b):
    return jnp.dot(a, b, preferred_element_type=F32)


def _dot_nt(a, b):
    return lax.dot_general(a, b, (((1,), (1,)), ((), ())), preferred_element_type=F32)


def _split_bf16(x):
    hi = x.astype(BF16)
    lo = (x - hi.astype(F32)).astype(BF16)
    return hi, lo


def _head_ones():
    r = lax.broadcasted_iota(jnp.int32, (LANES, LANES), 0) // HEAD_DIM
    c = lax.broadcasted_iota(jnp.int32, (LANES, LANES), 1) // HEAD_DIM
    return (r == c).astype(BF16)


def _head_sum(x, ones, exact):
    outs = []
    for c in range(x.shape[1] // LANES):
        xc = x[:, c * LANES:(c + 1) * LANES]
        if exact:
            hi, lo = _split_bf16(xc)
            outs.append(_dot(hi, ones) + _dot(lo, ones))
        else:
            outs.append(_dot(xc.astype(BF16), ones))
    return outs[0] if len(outs) == 1 else jnp.concatenate(outs, axis=1)


FFN_TM = 512
FFN_FK = 256


def _ffn_apply(x, g_ref, wg_ref, wu_ref, wd_ref):
    h = _rms(x, g_ref[...]).astype(BF16)
    d_ff = wg_ref.shape[1]
    acc = jnp.zeros(x.shape, F32)
    for c in range(d_ff // FFN_FK):
        sl = slice(c * FFN_FK, (c + 1) * FFN_FK)
        gate = _dot(h, wg_ref[:, sl])
        up = _dot(h, wu_ref[:, sl])
        act = (gate * jax.nn.sigmoid(gate) * up).astype(BF16)
        acc = acc + _dot(act, wd_ref[sl, :])
    return x + 0.5 * acc


def _ffn_specs(ffn):
    return [_resident(a.shape) for a in ffn]


def _ffn_kernel(x_ref, g_ref, wg_ref, wu_ref, wd_ref, o_ref):
    o_ref[...] = _ffn_apply(x_ref[...], g_ref, wg_ref, wu_ref, wd_ref)


def _ffn(x2, *ffn):
    m, d = x2.shape
    tm = min(FFN_TM, m)
    return pl.pallas_call(
        _ffn_kernel,
        out_shape=jax.ShapeDtypeStruct((m, d), F32),
        grid=(m // tm,),
        in_specs=[pl.BlockSpec((tm, d), lambda i: (i, 0))] + _ffn_specs(ffn),
        out_specs=pl.BlockSpec((tm, d), lambda i: (i, 0)),
        compiler_params=_params("parallel"),
        name="ffn",
    )(x2, *ffn)


PROJ_TM = 512
PROJ_NC = 256
SB_COLS = 3 * SB_HEADS * HEAD_DIM
DIL_COLS = DIL_HEADS * HEAD_DIM
GROUP_COLS = 3 * DIL_GROUP_HEADS * HEAD_DIM
DILATIONS = tuple(r for _, r in DIL_PATTERNS)
assert PROJ_NC == DIL_GROUP_HEADS * HEAD_DIM


def _proj_kernel(x_ref, g_ref, w_ref, qn_ref, kn_ref, nat_ref, *rest):
    stream_refs, pt_ref = rest[:-1], rest[-1]
    h = _rms(x_ref[...], g_ref[...]).astype(BF16)
    tm = x_ref.shape[0]
    ones = _head_ones()
    scale = HEAD_DIM ** -0.5
    streams = dict(zip([g for g, r in enumerate(DILATIONS) if r > 1], stream_refs))
    n_chunks = w_ref.shape[1] // PROJ_NC
    cols = lambda c: slice(c * PROJ_NC, (c + 1) * PROJ_NC)
    group_of = lambda c: c // 3 - 1

    def finish(c, p):
        part, grp = c % 3, group_of(c)
        if grp >= 0 and part < 2:
            norm_gain = qn_ref[...] if part == 0 else kn_ref[...]
            ms = _head_sum(p * p, ones, exact=False) * (1.0 / HEAD_DIM)
            p = p * lax.rsqrt(ms + NORM_EPS) * norm_gain
        if part == 0:
            p = p * (scale * LOG2E if grp < 0 else scale)
        if grp not in streams:
            nat_ref[:, cols(c)] = p.astype(BF16)
            return
        r = DILATIONS[grp]
        n = tm // r
        for j in range(PROJ_NC // LANES):
            buf = pt_ref.at[c % 2, j]
            buf[...] = p[:, j * LANES:(j + 1) * LANES]
            lanes = slice(part * PROJ_NC + j * LANES, part * PROJ_NC + (j + 1) * LANES)
            for cc in range(r):
                streams[grp][cc, :, lanes] = buf[pl.ds(cc, n, stride=r), :].astype(BF16)

    product = lambda c: _dot(h, w_ref[:, cols(c)])
    pending = product(0)
    for c in range(1, n_chunks):
        nxt = product(c)
        finish(c - 1, pending)
        pending = nxt
    finish(n_chunks - 1, pending)


def _proj(x3, g, w_in, q_norm, k_norm):
    b, s, d = x3.shape
    tm = min(PROJ_TM, s)
    tiles = s // tm
    assert all(DILATIONS[g] == 1 for g in range(len(DILATIONS)) if g < DILATIONS.count(1))
    group_w = DIL_GROUP_HEADS * HEAD_DIM
    pieces = [w_in[:, :SB_COLS]] + [
        w_in[:, SB_COLS + part * DIL_COLS + g * group_w:][:, :group_w]
        for g in range(len(DILATIONS)) for part in range(3)]
    w = jnp.concatenate(pieces, axis=1).astype(BF16)
    nat_w = SB_COLS + GROUP_COLS * DILATIONS.count(1)
    tile = lambda v: jnp.tile(v.astype(F32), PROJ_NC // HEAD_DIM).reshape(1, PROJ_NC)
    stream_rs = [r for r in DILATIONS if r > 1]
    outs = pl.pallas_call(
        _proj_kernel,
        out_shape=(jax.ShapeDtypeStruct((b, s, nat_w), BF16),)
        + tuple(jax.ShapeDtypeStruct((b, r, s // r, GROUP_COLS), BF16) for r in stream_rs),
        grid=(b, tiles),
        in_specs=[
            pl.BlockSpec((None, tm, d), lambda bi, t: (bi, t, 0)),
            _resident((1, d)),
            _resident(w.shape),
            _resident((1, PROJ_NC)),
            _resident((1, PROJ_NC)),
        ],
        out_specs=(pl.BlockSpec((None, tm, nat_w), lambda bi, t: (bi, t, 0)),)
        + tuple(pl.BlockSpec((None, r, tm // r, GROUP_COLS), lambda bi, t: (bi, 0, t, 0))
                for r in stream_rs),
        scratch_shapes=[pltpu.VMEM((2, PROJ_NC // LANES, tm, LANES), F32)],
        compiler_params=_params("parallel", "parallel"),
        name="attn_proj",
    )(x3, g.reshape(1, d), w, tile(q_norm), tile(k_norm))
    return outs[0], list(outs[1:])


SB_TQ = 512
SB_UNROLL = 4


def _sb_kernel(q_ref, k_ref, v_ref, o_ref, qh_ref, acc_ref, run_ref):
    qi = pl.program_id(2)
    tq = q_ref.shape[0]
    tk = ATT_BLOCK
    nsub = tq // tk
    lane = lax.broadcasted_iota(jnp.int32, (1, LANES), 1)
    head0 = lane < HEAD_DIM
    q = q_ref[...]
    zero = jnp.zeros_like(q)
    qh_ref[0] = jnp.where(head0, q, zero)
    qh_ref[1] = jnp.where(head0, zero, q)
    acc_ref[...] = jnp.zeros_like(acc_ref)
    run_ref[...] = jnp.zeros_like(run_ref)
    later = (lax.broadcasted_iota(jnp.int32, (tk, tk), 0)
             > lax.broadcasted_iota(jnp.int32, (tk, tk), 1)).astype(BF16)

    def step(blocks, masked):
        starts = [pl.multiple_of(j * tk, tk) for j, _ in blocks]
        kbs = [k_ref[pl.ds(st, tk), :] for st in starts]
        vbs = [v_ref[pl.ds(st, tk), :] for st in starts]
        chains = [(b, h) for b in range(len(blocks)) for h in range(2)]
        zs = [_dot_nt(qh_ref[h, blocks[b][1]:, :], kbs[b]) for b, h in chains]
        yield
        log_betas = [jnp.minimum(z, 0.0) - jnp.log(1.0 + jnp.exp2(-jnp.abs(z))) * LOG2E for z in zs]
        log_keeps = [lb - z for lb, z in zip(log_betas, zs)]
        if masked:
            stricts = [lax.broadcasted_iota(jnp.int32, z.shape, 1)
                       < lax.broadcasted_iota(jnp.int32, z.shape, 0) for z in zs]
            log_keeps = [jnp.where(s, lk, 0.0) for s, lk in zip(stricts, log_keeps)]
        afters = [_dot(lk.astype(BF16), later) for lk in log_keeps]
        totals = [jnp.sum(lk, axis=1, keepdims=True) for lk in log_keeps]
        yield
        for h in range(2):
            for b, (_, r0) in enumerate(blocks):
                c = 2 * b + h
                run = run_ref[h, r0:, :]
                w = jnp.exp2(log_betas[c] + afters[c] + run)
                if masked:
                    w = jnp.where(stricts[c], w, 0.0)
                run_ref[h, r0:, :] = run + totals[c]
                acc_ref[h, r0:, :] += _dot(w.astype(BF16), vbs[b])

    def run_skewed(gens):
        waiting, running = list(gens), []
        while waiting or running:
            if waiting:
                running.append(waiting.pop(0))
            for gen in list(running):
                if next(gen, "done") == "done":
                    running.remove(gen)

    run_skewed([step([(qi * nsub + c, c * tk) for c in reversed(range(nsub))], True)])

    def body(i, carry):
        j = qi * nsub - 1 - SB_UNROLL * i
        run_skewed([step([(j - u, 0)], False) for u in range(SB_UNROLL)])
        return carry

    lax.fori_loop(0, qi * (nsub // SB_UNROLL), body, 0)
    o_ref[...] = jnp.where(head0, acc_ref[0], acc_ref[1]).astype(o_ref.dtype)


def _sb_attention(qkv):
    b, s, _ = qkv.shape
    pairs = SB_HEADS * HEAD_DIM // LANES
    tq = min(SB_TQ, s)
    return pl.pallas_call(
        _sb_kernel,
        out_shape=jax.ShapeDtypeStruct((b, s, SB_HEADS * HEAD_DIM), BF16),
        grid=(b, pairs, s // tq),
        in_specs=[
            pl.BlockSpec((None, tq, LANES), lambda bi, p, i: (bi, i, p)),
            pl.BlockSpec((None, s, LANES), lambda bi, p, i: (bi, 0, pairs + p)),
            pl.BlockSpec((None, s, LANES), lambda bi, p, i: (bi, 0, 2 * pairs + p)),
        ],
        out_specs=pl.BlockSpec((None, tq, LANES), lambda bi, p, i: (bi, i, p)),
        scratch_shapes=[pltpu.VMEM((2, tq, LANES), BF16), pltpu.VMEM((2, tq, LANES), F32),
                        pltpu.VMEM((2, tq, LANES), F32)],
        compiler_params=_params("parallel", "parallel", "arbitrary"),
        name="sb_attn",
    )(qkv, qkv, qkv)


def _t5_bucket(dist):
    max_exact = N_BUCKETS // 2
    d = jnp.maximum(dist, 1).astype(F32)
    large = max_exact + (jnp.log(d / max_exact) / math.log(MAX_DISTANCE / max_exact)
                         * (N_BUCKETS - max_exact)).astype(jnp.int32)
    large = jnp.minimum(large, N_BUCKETS - 1)
    return jnp.where(dist < max_exact, dist, large)


def _bias_blocks(rel_bias_group, dilation):
    qi = jnp.arange(ATT_BLOCK)[:, None]
    kj = jnp.arange(2 * ATT_BLOCK)[None, :] - ATT_BLOCK
    dist = qi - kj
    bucket = _t5_bucket(jnp.maximum(dist, 0) * dilation)
    onehot = (bucket[None] == jnp.arange(N_BUCKETS)[:, None, None]).astype(F32)
    bias = jnp.einsum('nh,nqk->hqk', rel_bias_group.astype(F32), onehot,
                      precision=lax.Precision.HIGHEST)
    return bias.reshape(2, 2, ATT_BLOCK, 2 * ATT_BLOCK)


DIL_ROWS = 512


def _dil_kernel(q_ref, kp_ref, kc_ref, vp_ref, vc_ref, bias_ref, o_ref, lse_ref):
    n = pl.program_id(2)
    blk = ATT_BLOCK
    nb = q_ref.shape[0] // blk
    pairs = q_ref.shape[1] // LANES
    lane = lax.broadcasted_iota(jnp.int32, (1, LANES), 1)
    head0 = lane < HEAD_DIM
    row = lax.broadcasted_iota(jnp.int32, (blk, 2 * blk), 0)
    col = lax.broadcasted_iota(jnp.int32, (blk, 2 * blk), 1)
    dist = row - col + blk
    window = (dist >= 0) & (dist <= blk)
    first = window & ((n > 0) | (col >= blk))
    keys = jnp.concatenate([kp_ref[...], kc_ref[...]], axis=0)
    vals = jnp.concatenate([vp_ref[...], vc_ref[...]], axis=0)
    q = q_ref[...]
    zero = jnp.zeros_like(q)
    first_heads = jnp.tile(head0, (1, pairs))
    qh = (jnp.where(first_heads, q, zero), jnp.where(first_heads, zero, q))
    chains = [(i, p, h) for i in range(nb) for p in range(pairs) for h in range(2)]
    span = lambda i: slice(i * blk, (i + 2) * blk)
    lanes = lambda p: slice(p * LANES, (p + 1) * LANES)
    zs = [_dot_nt(qh[h][i * blk:(i + 1) * blk, lanes(p)], keys[span(i), lanes(p)])
          for i, p, h in chains]
    logits = [jnp.where(first if i == 0 else window, z + bias_ref[p, h], NEG_INF)
              for (i, p, h), z in zip(chains, zs)]
    ms = [jnp.max(l, axis=1, keepdims=True) for l in logits]
    ps = [jnp.exp(l - m) for l, m in zip(logits, ms)]
    dens = [jnp.sum(p_, axis=1, keepdims=True) for p_ in ps]
    outs = [_dot((p_ / den).astype(BF16), vals[span(i), lanes(p)])
            for (i, p, _), p_, den in zip(chains, ps, dens)]
    lses = [m + jnp.log(den) for m, den in zip(ms, dens)]
    for c in range(0, len(chains), 2):
        i, p, _ = chains[c]
        rows = slice(i * blk, (i + 1) * blk)
        o_ref[rows, lanes(p)] = jnp.where(head0, outs[c], outs[c + 1])
        lse_ref[rows, lanes(p)] = jnp.where(head0, lses[c], lses[c + 1])


def _dil_attention(src, col_blocks, bias):
    b, r, l, _ = src.shape
    width = DIL_GROUP_HEADS * HEAD_DIM
    blk = ATT_BLOCK
    rows = min(DIL_ROWS, l)
    nb = rows // blk
    assert all(off * LANES % width == 0 for off in col_blocks)
    qo, ko, vo = (off * LANES // width for off in col_blocks)
    cur = lambda off: (lambda bi, c, n: (bi, c, n, off))
    prev = lambda off: (lambda bi, c, n: (bi, c, jnp.maximum(n * nb - 1, 0), off))
    tile = (None, None, rows, width)
    one = (None, None, blk, width)
    out = jax.ShapeDtypeStruct((b, r, l, width), F32)
    return pl.pallas_call(
        _dil_kernel,
        out_shape=(out, out),
        grid=(b, r, l // rows),
        in_specs=[
            pl.BlockSpec(tile, cur(qo)),
            pl.BlockSpec(one, prev(ko)),
            pl.BlockSpec(tile, cur(ko)),
            pl.BlockSpec(one, prev(vo)),
            pl.BlockSpec(tile, cur(vo)),
            _resident(bias.shape),
        ],
        out_specs=(pl.BlockSpec(tile, cur(0)), pl.BlockSpec(tile, cur(0))),
        compiler_params=_params("parallel", "parallel", "arbitrary"),
        name="dil_attn",
    )(src, src, src, src, src, bias)


OUT_TM = 512


def _attn_out_kernel(x_ref, sb_ref, o0_ref, o1_ref, o2_ref, l0_ref, l1_ref, l2_ref, w_ref,
                     g_ref, wg_ref, wu_ref, wd_ref, y_ref, *order_refs):
    tm = x_ref.shape[0]
    scratch = list(order_refs)

    def natural(ref):
        r = ref.shape[0]
        if r == 1:
            return ref[0]
        buf = scratch.pop()
        for c in range(r):
            for j in range(buf.shape[0]):
                buf[j, pl.ds(c, tm // r, stride=r), :] = ref[c, :, j * LANES:(j + 1) * LANES]
        return jnp.concatenate([buf[j] for j in range(buf.shape[0])], axis=1)

    l0, l1, l2 = natural(l0_ref), natural(l1_ref), natural(l2_ref)
    mx = jnp.maximum(jnp.maximum(l0, l1), l2)
    e0, e1, e2 = jnp.exp(l0 - mx), jnp.exp(l1 - mx), jnp.exp(l2 - mx)
    den = e0 + e1 + e2
    out_b = ((e0 / den) * natural(o0_ref) + (e1 / den) * natural(o1_ref)
             + (e2 / den) * natural(o2_ref))
    na = sb_ref.shape[1]
    y = _dot(sb_ref[...], w_ref[:na, :]) + _dot(out_b.astype(BF16), w_ref[na:, :])
    y_ref[...] = _ffn_apply(x_ref[...] + y, g_ref, wg_ref, wu_ref, wd_ref)


def _attn_out(x3, sb, outs, lses, w_out, ffn):
    b, s, d = x3.shape
    tm = min(OUT_TM, s)
    wide = sb.shape[2]
    row = lambda w: pl.BlockSpec((None, tm, w), lambda bi, t: (bi, t, 0))
    stream = lambda a: pl.BlockSpec((None, a.shape[1], tm // a.shape[1], wide),
                                    lambda bi, t: (bi, 0, t, 0))
    n_buffers = 2 * sum(1 for a in outs if a.shape[1] > 1)
    return pl.pallas_call(
        _attn_out_kernel,
        out_shape=jax.ShapeDtypeStruct((b, s, d), F32),
        grid=(b, s // tm),
        in_specs=[row(d), row(wide)] + [stream(a) for a in (*outs, *lses)]
        + [_resident(w_out.shape)] + _ffn_specs(ffn),
        out_specs=row(d),
        scratch_shapes=[pltpu.VMEM((wide // LANES, tm, LANES), F32)] * n_buffers,
        compiler_params=_params("parallel", "parallel"),
        name="attn_out_ffn",
    )(x3, sb, *outs, *lses, w_out, *ffn)


RW_TS = 512


def _rwkv_in_kernel(x_ref, xp_ref, g_ref, mix_ref, wr_ref, wk_ref, wv_ref,
                    w1_ref, w2_ref, a1_ref, a2_ref, g1_ref, g2_ref,
                    r_ref, k_ref, v_ref, lw_ref, la_ref, gate_ref):
    si = pl.program_id(1)
    gain = g_ref[...]
    h = _rms(x_ref[...], gain)
    prev_last = _rms(xp_ref[...], gain)[SUBLANES - 1:, :] * (si > 0).astype(F32)
    rows = lax.broadcasted_iota(jnp.int32, h.shape, 0)
    shifted = jnp.where(rows == 0, prev_last, pltpu.roll(h, 1, axis=0))
    xx = shifted - h
    mixed = lambda i: (h + xx * mix_ref[i:i + 1, :]).astype(BF16)

    r_ref[...] = _dot(mixed(0), wr_ref[...]).astype(r_ref.dtype)
    k_ref[...] = _dot(mixed(2), wk_ref[...]).astype(k_ref.dtype)
    v_ref[...] = _dot(mixed(3), wv_ref[...]).astype(v_ref.dtype)
    lw_ref[...] = _dot(jnp.tanh(_dot(mixed(1), w1_ref[...])).astype(BF16), w2_ref[...])
    la_ref[...] = _dot(_dot(mixed(4), a1_ref[...]).astype(BF16), a2_ref[...]).astype(la_ref.dtype)
    gate_ref[...] = _dot(jax.nn.sigmoid(_dot(mixed(5), g1_ref[...])).astype(BF16),
                         g2_ref[...]).astype(gate_ref.dtype)


def _rwkv_in(x3, g, mix, wr, wk, wv, w1, w2, a1, a2, g1, g2):
    b, s, d = x3.shape
    ts = min(RW_TS, s)
    tile = pl.BlockSpec((None, ts, d), lambda bi, si: (bi, si, 0))
    prev = pl.BlockSpec((None, SUBLANES, d),
                        lambda bi, si: (bi, jnp.maximum(si * (ts // SUBLANES) - 1, 0), 0))
    out = lambda dt: jax.ShapeDtypeStruct((b, s, d), dt)
    weights = (wr, wk, wv, w1, w2, a1, a2, g1, g2)
    return pl.pallas_call(
        _rwkv_in_kernel,
        out_shape=(out(BF16), out(BF16), out(BF16), out(F32), out(BF16), out(BF16)),
        grid=(b, s // ts),
        in_specs=[tile, prev, _resident((1, d)), _resident(mix.shape)]
        + [_resident(w.shape) for w in weights],
        out_specs=(tile,) * 6,
        compiler_params=_params("parallel", "arbitrary"),
        name="rwkv_in",
    )(x3, x3, g.reshape(1, d), mix, *weights)


SCAN_ROWS = 1024
SCAN_GROUPS = 2
SCAN_SKEW = 2


def _scan_kernel(tiles_per_seq, r_ref, k_ref, v_ref, lw_ref, la_ref, vec_ref, y_ref, bonus_ref,
                 state_ref, qm_ref, y3_ref, n_ref):
    c_len = SCAN_CHUNK
    two = 2 * c_len
    n_chunks = r_ref.shape[0] // c_len
    step = pl.program_id(0)

    @pl.when(step == 0)
    def _():
        state_ref[...] = jnp.zeros_like(state_ref)
        qm_ref[...] = jnp.zeros_like(qm_ref)
        y3_ref[...] = jnp.zeros_like(y3_ref)
        n_ref[...] = jnp.zeros_like(n_ref)

    lane = lax.broadcasted_iota(jnp.int32, (1, LANES), 1)
    head0 = lane < HEAD_DIM
    ti = lax.broadcasted_iota(jnp.int32, (c_len, two), 0)
    ii = lax.broadcasted_iota(jnp.int32, (c_len, two), 1) % c_len
    strict, incl, ident = ti > ii, ti >= ii, ti == ii
    ri = lax.broadcasted_iota(jnp.int32, (LANES, LANES), 0)
    ci = lax.broadcasted_iota(jnp.int32, (LANES, LANES), 1)
    same_head = (ri // HEAD_DIM) == (ci // HEAD_DIM)
    eye = ri == ci
    tri = (lax.broadcasted_iota(jnp.int32, (c_len, c_len), 0)
           >= lax.broadcasted_iota(jnp.int32, (c_len, c_len), 1)).astype(BF16)
    tri2 = jnp.concatenate([tri, tri], axis=1)

    def stack(x):
        zero = jnp.zeros_like(x)
        return jnp.concatenate([jnp.where(head0, x, zero), jnp.where(head0, zero, x)], axis=0)

    bf = lambda x: x.astype(BF16)
    each = lambda f, *ls: [f(*xs) for xs in zip(*ls)]

    w0, a0, k_k, k_a, r_k = (vec_ref[i:i + 1, :] for i in range(5))
    ones = _head_ones()

    def token_maps(rw):
        r, k, v, la = (ref[rw, :].astype(F32) for ref in (r_ref, k_ref, v_ref, la_ref))
        t = -(w0 + lw_ref[rw, :])
        w_log = -(jnp.maximum(t, 0.0) + jnp.log(1.0 + jnp.exp(-jnp.abs(t)))) - 0.5
        a = jax.nn.sigmoid(a0 + la)
        return dict(r=r, v=v, a=a, kk=k * k_k, k=k * (1.0 + (a - 1.0) * k_a),
                    ld=-jnp.exp(w_log))

    def prepare(x, norm, cum):
        r, k, v, ld = x["r"], x["k"], x["v"], x["ld"]
        kk = x["kk"] / jnp.maximum(norm, 1e-12)
        na, bb = -kk, kk * x["a"]
        total = cum[c_len - 1:, :]
        e_neg = jnp.exp(-cum)
        e_tail = jnp.exp(total - cum)
        a_t = bf(na * jnp.exp(cum - ld))
        r_t = r * jnp.exp(cum)
        vb = bf(v)
        tails = jnp.concatenate([bb * e_tail, k * e_tail], axis=0)
        return dict(
            r_t=r_t, a_st=stack(a_t), vb=vb, v_st=stack(vb),
            lhs=jnp.concatenate([a_t, bf(r_t)], axis=0),
            rhs=jnp.concatenate([stack(bf(bb * e_neg)), stack(bf(k * e_neg))], axis=0),
            tails_t=bf(jnp.transpose(tails)),
            decay=jnp.where(eye, jnp.exp(total), 0.0))

    def transitions(rows, out):
        n = len(rows)
        xs = []
        for g, rw in enumerate(rows):
            xs.append(token_maps(rw))
            if g % 2:
                yield
        parts = [_split_bf16(x["ld"]) for x in xs]
        his = jnp.concatenate([p[0] for p in parts], axis=1) if n > 1 else parts[0][0]
        los = jnp.concatenate([p[1] for p in parts], axis=1) if n > 1 else parts[0][1]
        cum_all = _dot(tri2, jnp.concatenate([his, los], axis=0))
        cums = [cum_all[:, g * LANES:(g + 1) * LANES] for g in range(n)]
        by_chunk = lambda a: [a[g * c_len:(g + 1) * c_len] for g in range(n)]
        norms = by_chunk(jnp.sqrt(_dot(bf(jnp.concatenate([x["kk"] * x["kk"] for x in xs], axis=0)),
                                       ones)))
        hi, lo = _split_bf16(jnp.concatenate([x["r"] * x["k"] * r_k for x in xs], axis=0))
        for rw, x, wgt in zip(rows, xs, by_chunk(_dot(hi, ones) + _dot(lo, ones))):
            bonus_ref[rw, :] = wgt * x["v"]
        ps = each(prepare, xs, norms, cums)
        yield

        grams = [_dot_nt(p["lhs"], p["rhs"]) for p in ps]
        n_abs = [jnp.where(strict, g[:c_len, :two], 0.0) for g in grams]
        t1s = [_dot(bf(jnp.where(strict, g[:c_len, two:], 0.0)), p["v_st"])
               for g, p in zip(grams, ps)]
        b_rs = [bf(jnp.concatenate([jnp.where(incl, g[c_len:, :two], 0.0),
                                    jnp.where(incl, g[c_len:, two:], 0.0)], axis=1)) for g in grams]
        yield

        pws = [_dot(bf(x), stack(bf(x))) for x in n_abs]
        invs = [jnp.where(ident, 1.0, x) for x in n_abs]
        yield
        terms = 2
        while terms < c_len:
            last = 2 * terms >= c_len
            nxt_p, nxt_t = [], []
            for pw, inv in zip(pws, invs):
                pwb = bf(pw)
                if last:
                    nxt_t.append(inv + _dot(pwb, stack(bf(inv))))
                else:
                    both = _dot(pwb, jnp.concatenate([stack(pwb), stack(bf(inv))], axis=1))
                    nxt_p.append(both[:, :two])
                    nxt_t.append(inv + both[:, two:])
            pws, invs = nxt_p, nxt_t
            terms *= 2
            yield

        wus = [bf(_dot(bf(inv), jnp.concatenate([p["a_st"], stack(bf(t1))], axis=1)))
               for inv, p, t1 in zip(invs, ps, t1s)]
        yield
        zeros = jnp.zeros((c_len, LANES), BF16)
        tops = [_dot(p["tails_t"], jnp.concatenate(
                    [wu, jnp.concatenate([zeros, p["vb"]], axis=1)], axis=0))
                for p, wu in zip(ps, wus)]
        lows = [_dot(b_r, jnp.concatenate(
                    [jnp.concatenate([stack(wu[:, :LANES]), stack(wu[:, LANES:])], axis=1),
                     jnp.concatenate([jnp.zeros((two, LANES), BF16), p["v_st"]], axis=1)], axis=0))
                for b_r, p, wu in zip(b_rs, ps, wus)]
        for p, top, low in zip(ps, tops, lows):
            m_mat = p["decay"] + jnp.where(same_head, top[:, :LANES], 0.0)
            n_mat = jnp.where(same_head, top[:, LANES:], 0.0)
            q2 = p["r_t"] + low[:, :LANES]
            out.append((bf(jnp.concatenate([q2, m_mat], axis=0)), low[:, LANES:], n_mat))

    rows = [pl.ds(g * c_len, c_len) for g in range(n_chunks)]
    starts_seq = (step - 1) % tiles_per_seq == 0
    chain = {"state": jnp.where(starts_seq, 0.0, state_ref[...]), "next": 0}

    def chain_step():
        g = chain["next"]
        if g == n_chunks:
            return
        chain["next"] = g + 1
        res = _dot(qm_ref[g], bf(chain["state"]))
        y_ref[rows[g], :] = res[:c_len] + y3_ref[g]
        chain["state"] = res[c_len:] + n_ref[g]

    size = n_chunks // SCAN_GROUPS
    results = [[] for _ in range(SCAN_GROUPS)]
    waiting = [transitions(rows[i * size:(i + 1) * size], results[i]) for i in range(SCAN_GROUPS)]
    running, slot = [], 0
    while waiting or running:
        if waiting and slot % SCAN_SKEW == 0:
            running.append(waiting.pop(0))
        for gen in list(running):
            if next(gen, "done") == "done":
                running.remove(gen)
        chain_step()
        slot += 1
    while chain["next"] < n_chunks:
        chain_step()
    state_ref[...] = chain["state"]
    for g, (qm, y3, n_mat) in enumerate(sum(results, [])):
        qm_ref[g] = qm
        y3_ref[g] = y3
        n_ref[g] = n_mat


def _scan(r, k, v, lw, la, vecs):
    b, s, d = r.shape
    rows = min(SCAN_ROWS, s)
    assert s % rows == 0 and rows % (SCAN_GROUPS * SCAN_CHUNK) == 0
    pairs, tiles = d // LANES, s // rows
    n_tiles = b * pairs * tiles
    n_chunks = rows // SCAN_CHUNK

    def tile_of(j):
        return j // (pairs * tiles), j % tiles, (j // tiles) % pairs

    in_tile = pl.BlockSpec((None, rows, LANES), lambda i: tile_of(jnp.minimum(i, n_tiles - 1)))
    out_tile = pl.BlockSpec((None, rows, LANES), lambda i: tile_of(jnp.maximum(i - 1, 0)))
    vec_tile = pl.BlockSpec((vecs.shape[0], LANES),
                            lambda i: (0, tile_of(jnp.minimum(i, n_tiles - 1))[2]))
    out = jax.ShapeDtypeStruct((b, s, d), F32)
    return pl.pallas_call(
        functools.partial(_scan_kernel, tiles),
        out_shape=(out, out),
        grid=(n_tiles + 1,),
        in_specs=[in_tile] * 5 + [vec_tile],
        out_specs=(out_tile, in_tile),
        scratch_shapes=[pltpu.VMEM((LANES, LANES), F32),
                        pltpu.VMEM((n_chunks, SCAN_CHUNK + LANES, LANES), BF16),
                        pltpu.VMEM((n_chunks, SCAN_CHUNK, LANES), F32),
                        pltpu.VMEM((n_chunks, LANES, LANES), F32)],
        compiler_params=_params("arbitrary"),
        name="rwkv_scan",
    )(r, k, v, lw, la, vecs)


RW_OUT_TM = 512


def _rwkv_out_kernel(x_ref, y_ref, bonus_ref, gate_ref, lg_ref, lb_ref, wo_ref,
                     g_ref, wg_ref, wu_ref, wd_ref, o_ref):
    y = y_ref[...]
    ones = _head_ones()
    mu = _head_sum(y, ones, exact=True) * (1.0 / HEAD_DIM)
    dlt = y - mu
    var = _head_sum(dlt * dlt, ones, exact=False) * (1.0 / HEAD_DIM)
    yn = dlt * lax.rsqrt(var + GN_EPS) * lg_ref[...] + lb_ref[...] + bonus_ref[...]
    x = x_ref[...] + _dot((yn * gate_ref[...].astype(F32)).astype(BF16), wo_ref[...])
    o_ref[...] = _ffn_apply(x, g_ref, wg_ref, wu_ref, wd_ref)


def _rwkv_out(x2, y2, bonus2, gate2, lnx_g, lnx_b, wo, ffn):
    m, d = x2.shape
    tm = min(RW_OUT_TM, m)
    row = pl.BlockSpec((tm, d), lambda i: (i, 0))
    return pl.pallas_call(
        _rwkv_out_kernel,
        out_shape=jax.ShapeDtypeStruct((m, d), F32),
        grid=(m // tm,),
        in_specs=[row, row, row, row, _resident((1, d)), _resident((1, d)), _resident(wo.shape)]
        + _ffn_specs(ffn),
        out_specs=row,
        compiler_params=_params("parallel"),
        name="rwkv_out_ffn",
    )(x2, y2, bonus2, gate2, lnx_g.reshape(1, d), lnx_b.reshape(1, d), wo, *ffn)


def _attention_layer(x3, mix_g, rel_bias, w_in, q_norm, k_norm, w_out, ffn):
    b, s, d = x3.shape
    nat, streams = _proj(x3, mix_g, w_in, q_norm, k_norm)
    sb = _sb_attention(nat)
    outs, lses = [], []
    group_w = DIL_GROUP_HEADS * HEAD_DIM
    for g, r in enumerate(DILATIONS):
        if r == 1:
            src = nat.reshape(b, 1, s, -1)
            base = SB_COLS + GROUP_COLS * g
        else:
            src, base = streams.pop(0), 0
        offs = tuple((base + part * group_w) // LANES for part in range(3))
        bias = _bias_blocks(rel_bias[:, g * DIL_GROUP_HEADS:(g + 1) * DIL_GROUP_HEADS], r)
        o, l = _dil_attention(src, offs, bias)
        outs.append(o)
        lses.append(l)
    return _attn_out(x3, sb, outs, lses, w_out.astype(BF16), ffn)


def _pad_cols(w, n):
    return jnp.pad(w, ((0, 0), (0, n - w.shape[1])))


def _pad_rows(w, n):
    return jnp.pad(w, ((0, n - w.shape[0]), (0, 0)))


def _rwkv_layer(x3, mix_g, mix, w0, w1, w2, a0, a1, a2, g1, g2, k_k, k_a, r_k,
                w_r, w_k, w_v, w_o, lnx_g, lnx_b, ffn):
    b, s, d = x3.shape
    bf = lambda w: w.astype(BF16)
    lora = lambda w: -(-w // LANES) * LANES
    vecs = jnp.stack([w0, a0, k_k, k_a, r_k.reshape(d), jnp.zeros_like(w0),
                      jnp.zeros_like(w0), jnp.zeros_like(w0)], axis=0).astype(F32)
    dw, da, dg = lora(w1.shape[1]), lora(a1.shape[1]), lora(g1.shape[1])
    r, k, v, lw, la, gate = _rwkv_in(
        x3, mix_g, mix, bf(w_r), bf(w_k), bf(w_v),
        bf(_pad_cols(w1, dw)), bf(_pad_rows(w2, dw)),
        bf(_pad_cols(a1, da)), bf(_pad_rows(a2, da)),
        bf(_pad_cols(g1, dg)), bf(_pad_rows(g2, dg)))
    y, bonus = _scan(r, k, v, lw, la, vecs)
    flat = lambda t: t.reshape(b * s, d)
    out = _rwkv_out(flat(x3), flat(y), flat(bonus), flat(gate), lnx_g, lnx_b, bf(w_o), ffn)
    return out.reshape(b, s, d)


def kernel(x, ffn_norm, ffn_w_gate, ffn_w_up, ffn_w_down, mix_norm, rel_bias, attn_w_in, attn_q_norm, attn_k_norm, attn_w_out, rw_mix, rw_w0, rw_w1, rw_w2, rw_a0, rw_a1, rw_a2, rw_g1, rw_g2, rw_kk, rw_ka, rw_rk, rw_wr, rw_wk, rw_wv, rw_wo, rw_lnx_g, rw_lnx_b):
    b, s, d = x.shape
    depth = ffn_norm.shape[0]

    def ffn_operands(layer, half):
        return (ffn_norm[layer, half].reshape(1, d).astype(F32),
                ffn_w_gate[layer, half].astype(BF16), ffn_w_up[layer, half].astype(BF16),
                ffn_w_down[layer, half].astype(BF16))

    for layer in range(depth):
        x = _ffn(x.reshape(b * s, d), *ffn_operands(layer, 0)).reshape(b, s, d)
        second = ffn_operands(layer, 1)
        if layer % 2 == 0:
            e = layer // 2
            x = _attention_layer(x, mix_norm[layer], rel_bias, attn_w_in[e], attn_q_norm[e],
                                 attn_k_norm[e], attn_w_out[e], second)
        else:
            o = layer // 2
            x = _rwkv_layer(x, mix_norm[layer], rw_mix[o], rw_w0[o], rw_w1[o], rw_w2[o],
                            rw_a0[o], rw_a1[o], rw_a2[o], rw_g1[o], rw_g2[o], rw_kk[o],
                            rw_ka[o], rw_rk[o], rw_wr[o], rw_wk[o], rw_wv[o], rw_wo[o],
                            rw_lnx_g[o], rw_lnx_b[o], second)
    return x
```

```python
import functools
import math

import jax
import jax.numpy as jnp
from jax import lax
from jax.experimental import pallas as pl
from jax.experimental.pallas import tpu as pltpu

F32 = jnp.float32
BF16 = jnp.bfloat16

HEAD_DIM = 64
LANES = 128
SUBLANES = 8
SB_HEADS = 4
DIL_HEADS = 12
DIL_PATTERNS = ((128, 1), (512, 4), (2048, 16))
DIL_GROUP_HEADS = 4
ATT_BLOCK = 128
N_BUCKETS = 32
MAX_DISTANCE = 2048
NORM_EPS = 1e-6
GN_EPS = 64e-5
NEG_INF = -1e30
LOG2E = math.log2(math.e)
SCAN_CHUNK = 64
VMEM_LIMIT = 56 * 1024 * 1024


def _params(*sem):
    return pltpu.CompilerParams(dimension_semantics=sem, vmem_limit_bytes=VMEM_LIMIT)


def _resident(shape):
    nd = len(shape)
    return pl.BlockSpec(shape, lambda *_: (0,) * nd, pipeline_mode=pl.Buffered(1))


def _rms(x, g):
    ms = jnp.mean(x * x, axis=-1, keepdims=True)
    return x * lax.rsqrt(ms + NORM_EPS) * g


def _dot(a, b):
    return jnp.dot(a, b, preferred_element_type=F32)


def _dot_nt(a, b):
    return lax.dot_general(a, b, (((1,), (1,)), ((), ())), preferred_element_type=F32)


def _split_bf16(x):
    hi = x.astype(BF16)
    lo = (x - hi.astype(F32)).astype(BF16)
    return hi, lo


def _head_ones():
    r = lax.broadcasted_iota(jnp.int32, (LANES, LANES), 0) // HEAD_DIM
    c = lax.broadcasted_iota(jnp.int32, (LANES, LANES), 1) // HEAD_DIM
    return (r == c).astype(BF16)


def _head_sum(x, ones, exact):
    outs = []
    for c in range(x.shape[1] // LANES):
        xc = x[:, c * LANES:(c + 1) * LANES]
        if exact:
            hi, lo = _split_bf16(xc)
            outs.append(_dot(hi, ones) + _dot(lo, ones))
        else:
            outs.append(_dot(xc.astype(BF16), ones))
    return outs[0] if len(outs) == 1 else jnp.concatenate(outs, axis=1)


FFN_TM = 512
FFN_FK = 256


def _ffn_apply(x, g_ref, wg_ref, wu_ref, wd_ref):
    h = _rms(x, g_ref[...]).astype(BF16)
    d_ff = wg_ref.shape[1]
    acc = jnp.zeros(x.shape, F32)
    for c in range(d_ff // FFN_FK):
        sl = slice(c * FFN_FK, (c + 1) * FFN_FK)
        gate = _dot(h, wg_ref[:, sl])
        up = _dot(h, wu_ref[:, sl])
        act = (gate * jax.nn.sigmoid(gate) * up).astype(BF16)
        acc = acc + _dot(act, wd_ref[sl, :])
    return x + 0.5 * acc


def _ffn_specs(ffn):
    return [_resident(a.shape) for a in ffn]


def _ffn_kernel(x_ref, g_ref, wg_ref, wu_ref, wd_ref, o_ref):
    o_ref[...] = _ffn_apply(x_ref[...], g_ref, wg_ref, wu_ref, wd_ref)


def _ffn(x2, *ffn):
    m, d = x2.shape
    tm = min(FFN_TM, m)
    return pl.pallas_call(
        _ffn_kernel,
        out_shape=jax.ShapeDtypeStruct((m, d), F32),
        grid=(m // tm,),
        in_specs=[pl.BlockSpec((tm, d), lambda i: (i, 0))] + _ffn_specs(ffn),
        out_specs=pl.BlockSpec((tm, d), lambda i: (i, 0)),
        compiler_params=_params("parallel"),
        name="ffn",
    )(x2, *ffn)


PROJ_TM = 512
PROJ_NC = 256
SB_COLS = 3 * SB_HEADS * HEAD_DIM
DIL_COLS = DIL_HEADS * HEAD_DIM
GROUP_COLS = 3 * DIL_GROUP_HEADS * HEAD_DIM
DILATIONS = tuple(r for _, r in DIL_PATTERNS)
assert PROJ_NC == DIL_GROUP_HEADS * HEAD_DIM


def _proj_kernel(x_ref, g_ref, w_ref, qn_ref, kn_ref, nat_ref, *rest):
    stream_refs, pt_ref = rest[:-1], rest[-1]
    h = _rms(x_ref[...], g_ref[...]).astype(BF16)
    tm = x_ref.shape[0]
    ones = _head_ones()
    scale = HEAD_DIM ** -0.5
    streams = dict(zip([g for g, r in enumerate(DILATIONS) if r > 1], stream_refs))
    n_chunks = w_ref.shape[1] // PROJ_NC
    cols = lambda c: slice(c * PROJ_NC, (c + 1) * PROJ_NC)
    group_of = lambda c: c // 3 - 1

    def finish(c, p):
        part, grp = c % 3, group_of(c)
        if grp >= 0 and part < 2:
            norm_gain = qn_ref[...] if part == 0 else kn_ref[...]
            ms = _head_sum(p * p, ones, exact=False) * (1.0 / HEAD_DIM)
            p = p * lax.rsqrt(ms + NORM_EPS) * norm_gain
        if part == 0:
            p = p * (scale * LOG2E if grp < 0 else scale)
        if grp not in streams:
            nat_ref[:, cols(c)] = p.astype(BF16)
            return
        r = DILATIONS[grp]
        n = tm // r
        for j in range(PROJ_NC // LANES):
            buf = pt_ref.at[c % 2, j]
            buf[...] = p[:, j * LANES:(j + 1) * LANES]
            lanes = slice(part * PROJ_NC + j * LANES, part * PROJ_NC + (j + 1) * LANES)
            for cc in range(r):
                streams[grp][cc, :, lanes] = buf[pl.ds(cc, n, stride=r), :].astype(BF16)

    product = lambda c: _dot(h, w_ref[:, cols(c)])
    pending = product(0)
    for c in range(1, n_chunks):
        nxt = product(c)
        finish(c - 1, pending)
        pending = nxt
    finish(n_chunks - 1, pending)


def _proj(x3, g, w_in, q_norm, k_norm):
    b, s, d = x3.shape
    tm = min(PROJ_TM, s)
    tiles = s // tm
    assert all(DILATIONS[g] == 1 for g in range(len(DILATIONS)) if g < DILATIONS.count(1))
    group_w = DIL_GROUP_HEADS * HEAD_DIM
    pieces = [w_in[:, :SB_COLS]] + [
        w_in[:, SB_COLS + part * DIL_COLS + g * group_w:][:, :group_w]
        for g in range(len(DILATIONS)) for part in range(3)]
    w = jnp.concatenate(pieces, axis=1).astype(BF16)
    nat_w = SB_COLS + GROUP_COLS * DILATIONS.count(1)
    tile = lambda v: jnp.tile(v.astype(F32), PROJ_NC // HEAD_DIM).reshape(1, PROJ_NC)
    stream_rs = [r for r in DILATIONS if r > 1]
    outs = pl.pallas_call(
        _proj_kernel,
        out_shape=(jax.ShapeDtypeStruct((b, s, nat_w), BF16),)
        + tuple(jax.ShapeDtypeStruct((b, r, s // r, GROUP_COLS), BF16) for r in stream_rs),
        grid=(b, tiles),
        in_specs=[
            pl.BlockSpec((None, tm, d), lambda bi, t: (bi, t, 0)),
            _resident((1, d)),
            _resident(w.shape),
            _resident((1, PROJ_NC)),
            _resident((1, PROJ_NC)),
        ],
        out_specs=(pl.BlockSpec((None, tm, nat_w), lambda bi, t: (bi, t, 0)),)
        + tuple(pl.BlockSpec((None, r, tm // r, GROUP_COLS), lambda bi, t: (bi, 0, t, 0))
                for r in stream_rs),
        scratch_shapes=[pltpu.VMEM((2, PROJ_NC // LANES, tm, LANES), F32)],
        compiler_params=_params("parallel", "parallel"),
        name="attn_proj",
    )(x3, g.reshape(1, d), w, tile(q_norm), tile(k_norm))
    return outs[0], list(outs[1:])


SB_TQ = 512
SB_UNROLL = 8


def _sb_kernel(q_ref, k_ref, v_ref, o_ref, qh_ref, acc_ref, run_ref):
    qi = pl.program_id(2)
    tq = q_ref.shape[0]
    tk = ATT_BLOCK
    nsub = tq // tk
    lane = lax.broadcasted_iota(jnp.int32, (1, LANES), 1)
    head0 = lane < HEAD_DIM
    q = q_ref[...]
    zero = jnp.zeros_like(q)
    qh_ref[0] = jnp.where(head0, q, zero)
    qh_ref[1] = jnp.where(head0, zero, q)
    acc_ref[...] = jnp.zeros_like(acc_ref)
    run_ref[...] = jnp.zeros_like(run_ref)
    later = (lax.broadcasted_iota(jnp.int32, (tk, tk), 0)
             > lax.broadcasted_iota(jnp.int32, (tk, tk), 1)).astype(BF16)

    def step(blocks, masked):
        starts = [pl.multiple_of(j * tk, tk) for j, _ in blocks]
        kbs = [k_ref[pl.ds(st, tk), :] for st in starts]
        vbs = [v_ref[pl.ds(st, tk), :] for st in starts]
        chains = [(b, h) for b in range(len(blocks)) for h in range(2)]
        zs = [_dot_nt(qh_ref[h, blocks[b][1]:, :], kbs[b]) for b, h in chains]
        yield
        log_betas = [jnp.minimum(z, 0.0) - jnp.log(1.0 + jnp.exp2(-jnp.abs(z))) * LOG2E for z in zs]
        log_keeps = [lb - z for lb, z in zip(log_betas, zs)]
        if masked:
            stricts = [lax.broadcasted_iota(jnp.int32, z.shape, 1)
                       < lax.broadcasted_iota(jnp.int32, z.shape, 0) for z in zs]
            log_keeps = [jnp.where(s, lk, 0.0) for s, lk in zip(stricts, log_keeps)]
        afters = [_dot(lk.astype(BF16), later) for lk in log_keeps]
        totals = [jnp.sum(lk, axis=1, keepdims=True) for lk in log_keeps]
        yield
        for h in range(2):
            for b, (_, r0) in enumerate(blocks):
                c = 2 * b + h
                run = run_ref[h, r0:, :]
                w = jnp.exp2(log_betas[c] + afters[c] + run)
                if masked:
                    w = jnp.where(stricts[c], w, 0.0)
                run_ref[h, r0:, :] = run + totals[c]
                acc_ref[h, r0:, :] += _dot(w.astype(BF16), vbs[b])

    def run_skewed(gens):
        waiting, running = list(gens), []
        while waiting or running:
            if waiting:
                running.append(waiting.pop(0))
            for gen in list(running):
                if next(gen, "done") == "done":
                    running.remove(gen)

    run_skewed([step([(qi * nsub + c, c * tk) for c in reversed(range(nsub))], True)])

    def trip(j, n_blocks):
        run_skewed([step([(j - u, 0)], False) for u in range(n_blocks)])

    assert SB_UNROLL == 2 * nsub
    odd = qi % 2

    @pl.when(odd == 1)
    def _():
        trip(qi * nsub - 1, nsub)

    def body(i, carry):
        trip(qi * nsub - 1 - nsub * odd - SB_UNROLL * i, SB_UNROLL)
        return carry

    lax.fori_loop(0, qi // 2, body, 0)
    o_ref[...] = jnp.where(head0, acc_ref[0], acc_ref[1]).astype(o_ref.dtype)


def _sb_attention(qkv):
    b, s, _ = qkv.shape
    pairs = SB_HEADS * HEAD_DIM // LANES
    tq = min(SB_TQ, s)
    return pl.pallas_call(
        _sb_kernel,
        out_shape=jax.ShapeDtypeStruct((b, s, SB_HEADS * HEAD_DIM), BF16),
        grid=(b, pairs, s // tq),
        in_specs=[
            pl.BlockSpec((None, tq, LANES), lambda bi, p, i: (bi, i, p)),
            pl.BlockSpec((None, s, LANES), lambda bi, p, i: (bi, 0, pairs + p)),
            pl.BlockSpec((None, s, LANES), lambda bi, p, i: (bi, 0, 2 * pairs + p)),
        ],
        out_specs=pl.BlockSpec((None, tq, LANES), lambda bi, p, i: (bi, i, p)),
        scratch_shapes=[pltpu.VMEM((2, tq, LANES), BF16), pltpu.VMEM((2, tq, LANES), F32),
                        pltpu.VMEM((2, tq, LANES), F32)],
        compiler_params=_params("parallel", "parallel", "arbitrary"),
        name="sb_attn",
    )(qkv, qkv, qkv)


def _t5_bucket(dist):
    max_exact = N_BUCKETS // 2
    d = jnp.maximum(dist, 1).astype(F32)
    large = max_exact + (jnp.log(d / max_exact) / math.log(MAX_DISTANCE / max_exact)
                         * (N_BUCKETS - max_exact)).astype(jnp.int32)
    large = jnp.minimum(large, N_BUCKETS - 1)
    return jnp.where(dist < max_exact, dist, large)


def _bias_blocks(rel_bias_group, dilation):
    qi = jnp.arange(ATT_BLOCK)[:, None]
    kj = jnp.arange(2 * ATT_BLOCK)[None, :] - ATT_BLOCK
    dist = qi - kj
    bucket = _t5_bucket(jnp.maximum(dist, 0) * dilation)
    onehot = (bucket[None] == jnp.arange(N_BUCKETS)[:, None, None]).astype(F32)
    bias = jnp.einsum('nh,nqk->hqk', rel_bias_group.astype(F32), onehot,
                      precision=lax.Precision.HIGHEST)
    return bias.reshape(2, 2, ATT_BLOCK, 2 * ATT_BLOCK)


DIL_ROWS = 512


def _dil_kernel(q_ref, kp_ref, kc_ref, vp_ref, vc_ref, bias_ref, o_ref, lse_ref):
    n = pl.program_id(2)
    blk = ATT_BLOCK
    nb = q_ref.shape[0] // blk
    pairs = q_ref.shape[1] // LANES
    lane = lax.broadcasted_iota(jnp.int32, (1, LANES), 1)
    head0 = lane < HEAD_DIM
    row = lax.broadcasted_iota(jnp.int32, (blk, 2 * blk), 0)
    col = lax.broadcasted_iota(jnp.int32, (blk, 2 * blk), 1)
    dist = row - col + blk
    window = (dist >= 0) & (dist <= blk)
    first = window & ((n > 0) | (col >= blk))
    keys = jnp.concatenate([kp_ref[...], kc_ref[...]], axis=0)
    vals = jnp.concatenate([vp_ref[...], vc_ref[...]], axis=0)
    q = q_ref[...]
    zero = jnp.zeros_like(q)
    first_heads = jnp.tile(head0, (1, pairs))
    qh = (jnp.where(first_heads, q, zero), jnp.where(first_heads, zero, q))
    chains = [(i, p, h) for i in range(nb) for p in range(pairs) for h in range(2)]
    span = lambda i: slice(i * blk, (i + 2) * blk)
    lanes = lambda p: slice(p * LANES, (p + 1) * LANES)
    zs = [_dot_nt(qh[h][i * blk:(i + 1) * blk, lanes(p)], keys[span(i), lanes(p)])
          for i, p, h in chains]
    logits = [jnp.where(first if i == 0 else window, z + bias_ref[p, h], NEG_INF)
              for (i, p, h), z in zip(chains, zs)]
    ms = [jnp.max(l, axis=1, keepdims=True) for l in logits]
    ps = [jnp.exp(l - m) for l, m in zip(logits, ms)]
    dens = [jnp.sum(p_, axis=1, keepdims=True) for p_ in ps]
    outs = [_dot((p_ / den).astype(BF16), vals[span(i), lanes(p)])
            for (i, p, _), p_, den in zip(chains, ps, dens)]
    lses = [m + jnp.log(den) for m, den in zip(ms, dens)]
    for c in range(0, len(chains), 2):
        i, p, _ = chains[c]
        rows = slice(i * blk, (i + 1) * blk)
        o_ref[rows, lanes(p)] = jnp.where(head0, outs[c], outs[c + 1])
        lse_ref[rows, lanes(p)] = jnp.where(head0, lses[c], lses[c + 1])


def _dil_attention(src, col_blocks, bias):
    b, r, l, _ = src.shape
    width = DIL_GROUP_HEADS * HEAD_DIM
    blk = ATT_BLOCK
    rows = min(DIL_ROWS, l)
    nb = rows // blk
    assert all(off * LANES % width == 0 for off in col_blocks)
    qo, ko, vo = (off * LANES // width for off in col_blocks)
    cur = lambda off: (lambda bi, c, n: (bi, c, n, off))
    prev = lambda off: (lambda bi, c, n: (bi, c, jnp.maximum(n * nb - 1, 0), off))
    tile = (None, None, rows, width)
    one = (None, None, blk, width)
    out = jax.ShapeDtypeStruct((b, r, l, width), F32)
    return pl.pallas_call(
        _dil_kernel,
        out_shape=(out, out),
        grid=(b, r, l // rows),
        in_specs=[
            pl.BlockSpec(tile, cur(qo)),
            pl.BlockSpec(one, prev(ko)),
            pl.BlockSpec(tile, cur(ko)),
            pl.BlockSpec(one, prev(vo)),
            pl.BlockSpec(tile, cur(vo)),
            _resident(bias.shape),
        ],
        out_specs=(pl.BlockSpec(tile, cur(0)), pl.BlockSpec(tile, cur(0))),
        compiler_params=_params("parallel", "parallel", "arbitrary"),
        name="dil_attn",
    )(src, src, src, src, src, bias)


OUT_TM = 512


def _attn_out_kernel(x_ref, sb_ref, o0_ref, o1_ref, o2_ref, l0_ref, l1_ref, l2_ref, w_ref,
                     g_ref, wg_ref, wu_ref, wd_ref, y_ref, *order_refs):
    tm = x_ref.shape[0]
    scratch = list(order_refs)

    def natural(ref):
        r = ref.shape[0]
        if r == 1:
            return ref[0]
        buf = scratch.pop()
        for c in range(r):
            for j in range(buf.shape[0]):
                buf[j, pl.ds(c, tm // r, stride=r), :] = ref[c, :, j * LANES:(j + 1) * LANES]
        return jnp.concatenate([buf[j] for j in range(buf.shape[0])], axis=1)

    l0, l1, l2 = natural(l0_ref), natural(l1_ref), natural(l2_ref)
    mx = jnp.maximum(jnp.maximum(l0, l1), l2)
    e0, e1, e2 = jnp.exp(l0 - mx), jnp.exp(l1 - mx), jnp.exp(l2 - mx)
    den = e0 + e1 + e2
    out_b = ((e0 / den) * natural(o0_ref) + (e1 / den) * natural(o1_ref)
             + (e2 / den) * natural(o2_ref))
    na = sb_ref.shape[1]
    y = _dot(sb_ref[...], w_ref[:na, :]) + _dot(out_b.astype(BF16), w_ref[na:, :])
    y_ref[...] = _ffn_apply(x_ref[...] + y, g_ref, wg_ref, wu_ref, wd_ref)


def _attn_out(x3, sb, outs, lses, w_out, ffn):
    b, s, d = x3.shape
    tm = min(OUT_TM, s)
    wide = sb.shape[2]
    row = lambda w: pl.BlockSpec((None, tm, w), lambda bi, t: (bi, t, 0))
    stream = lambda a: pl.BlockSpec((None, a.shape[1], tm // a.shape[1], wide),
                                    lambda bi, t: (bi, 0, t, 0))
    n_buffers = 2 * sum(1 for a in outs if a.shape[1] > 1)
    return pl.pallas_call(
        _attn_out_kernel,
        out_shape=jax.ShapeDtypeStruct((b, s, d), F32),
        grid=(b, s // tm),
        in_specs=[row(d), row(wide)] + [stream(a) for a in (*outs, *lses)]
        + [_resident(w_out.shape)] + _ffn_specs(ffn),
        out_specs=row(d),
        scratch_shapes=[pltpu.VMEM((wide // LANES, tm, LANES), F32)] * n_buffers,
        compiler_params=_params("parallel", "parallel"),
        name="attn_out_ffn",
    )(x3, sb, *outs, *lses, w_out, *ffn)


RW_TS = 512


def _rwkv_in_kernel(x_ref, xp_ref, g_ref, mix_ref, wr_ref, wk_ref, wv_ref,
                    w1_ref, w2_ref, a1_ref, a2_ref, g1_ref, g2_ref,
                    r_ref, k_ref, v_ref, lw_ref, la_ref, gate_ref):
    si = pl.program_id(1)
    gain = g_ref[...]
    h = _rms(x_ref[...], gain)
    prev_last = _rms(xp_ref[...], gain)[SUBLANES - 1:, :] * (si > 0).astype(F32)
    rows = lax.broadcasted_iota(jnp.int32, h.shape, 0)
    shifted = jnp.where(rows == 0, prev_last, pltpu.roll(h, 1, axis=0))
    xx = shifted - h
    mixed = lambda i: (h + xx * mix_ref[i:i + 1, :]).astype(BF16)

    r_ref[...] = _dot(mixed(0), wr_ref[...]).astype(r_ref.dtype)
    k_ref[...] = _dot(mixed(2), wk_ref[...]).astype(k_ref.dtype)
    v_ref[...] = _dot(mixed(3), wv_ref[...]).astype(v_ref.dtype)
    lw_ref[...] = _dot(jnp.tanh(_dot(mixed(1), w1_ref[...])).astype(BF16), w2_ref[...])
    la_ref[...] = _dot(_dot(mixed(4), a1_ref[...]).astype(BF16), a2_ref[...]).astype(la_ref.dtype)
    gate_ref[...] = _dot(jax.nn.sigmoid(_dot(mixed(5), g1_ref[...])).astype(BF16),
                         g2_ref[...]).astype(gate_ref.dtype)


def _rwkv_in(x3, g, mix, wr, wk, wv, w1, w2, a1, a2, g1, g2):
    b, s, d = x3.shape
    ts = min(RW_TS, s)
    tile = pl.BlockSpec((None, ts, d), lambda bi, si: (bi, si, 0))
    prev = pl.BlockSpec((None, SUBLANES, d),
                        lambda bi, si: (bi, jnp.maximum(si * (ts // SUBLANES) - 1, 0), 0))
    out = lambda dt: jax.ShapeDtypeStruct((b, s, d), dt)
    weights = (wr, wk, wv, w1, w2, a1, a2, g1, g2)
    return pl.pallas_call(
        _rwkv_in_kernel,
        out_shape=(out(BF16), out(BF16), out(BF16), out(F32), out(BF16), out(BF16)),
        grid=(b, s // ts),
        in_specs=[tile, prev, _resident((1, d)), _resident(mix.shape)]
        + [_resident(w.shape) for w in weights],
        out_specs=(tile,) * 6,
        compiler_params=_params("parallel", "arbitrary"),
        name="rwkv_in",
    )(x3, x3, g.reshape(1, d), mix, *weights)


SCAN_ROWS = 1024
SCAN_GROUPS = 2
SCAN_SKEW = 2


def _scan_kernel(tiles_per_seq, r_ref, k_ref, v_ref, lw_ref, la_ref, vec_ref, y_ref, bonus_ref,
                 state_ref, qm_ref, y3_ref, n_ref):
    c_len = SCAN_CHUNK
    two = 2 * c_len
    n_chunks = r_ref.shape[0] // c_len
    step = pl.program_id(0)

    @pl.when(step == 0)
    def _():
        state_ref[...] = jnp.zeros_like(state_ref)
        qm_ref[...] = jnp.zeros_like(qm_ref)
        y3_ref[...] = jnp.zeros_like(y3_ref)
        n_ref[...] = jnp.zeros_like(n_ref)

    lane = lax.broadcasted_iota(jnp.int32, (1, LANES), 1)
    head0 = lane < HEAD_DIM
    ti = lax.broadcasted_iota(jnp.int32, (c_len, two), 0)
    ii = lax.broadcasted_iota(jnp.int32, (c_len, two), 1) % c_len
    strict, incl, ident = ti > ii, ti >= ii, ti == ii
    ri = lax.broadcasted_iota(jnp.int32, (LANES, LANES), 0)
    ci = lax.broadcasted_iota(jnp.int32, (LANES, LANES), 1)
    same_head = (ri // HEAD_DIM) == (ci // HEAD_DIM)
    eye = ri == ci
    tri = (lax.broadcasted_iota(jnp.int32, (c_len, c_len), 0)
           >= lax.broadcasted_iota(jnp.int32, (c_len, c_len), 1)).astype(BF16)
    tri2 = jnp.concatenate([tri, tri], axis=1)

    def stack(x):
        zero = jnp.zeros_like(x)
        return jnp.concatenate([jnp.where(head0, x, zero), jnp.where(head0, zero, x)], axis=0)

    bf = lambda x: x.astype(BF16)
    each = lambda f, *ls: [f(*xs) for xs in zip(*ls)]

    w0, a0, k_k, k_a, r_k = (vec_ref[i:i + 1, :] for i in range(5))
    ones = _head_ones()

    def token_maps(rw):
        r, k, v, la = (ref[rw, :].astype(F32) for ref in (r_ref, k_ref, v_ref, la_ref))
        t = -(w0 + lw_ref[rw, :])
        w_log = -(jnp.maximum(t, 0.0) + jnp.log(1.0 + jnp.exp(-jnp.abs(t)))) - 0.5
        a = jax.nn.sigmoid(a0 + la)
        return dict(r=r, v=v, a=a, kk=k * k_k, k=k * (1.0 + (a - 1.0) * k_a),
                    ld=-jnp.exp(w_log))

    def prepare(x, norm, cum):
        r, k, v, ld = x["r"], x["k"], x["v"], x["ld"]
        kk = x["kk"] / jnp.maximum(norm, 1e-12)
        na, bb = -kk, kk * x["a"]
        total = cum[c_len - 1:, :]
        e_neg = jnp.exp(-cum)
        e_tail = jnp.exp(total - cum)
        a_t = bf(na * jnp.exp(cum - ld))
        r_t = r * jnp.exp(cum)
        vb = bf(v)
        tails = jnp.concatenate([bb * e_tail, k * e_tail], axis=0)
        return dict(
            r_t=r_t, a_st=stack(a_t), vb=vb, v_st=stack(vb),
            lhs=jnp.concatenate([a_t, bf(r_t)], axis=0),
            rhs=jnp.concatenate([stack(bf(bb * e_neg)), stack(bf(k * e_neg))], axis=0),
            tails_t=bf(jnp.transpose(tails)),
            decay=jnp.where(eye, jnp.exp(total), 0.0))

    def transitions(rows, out):
        n = len(rows)
        xs = []
        for g, rw in enumerate(rows):
            xs.append(token_maps(rw))
            if g % 2:
                yield
        parts = [_split_bf16(x["ld"]) for x in xs]
        his = jnp.concatenate([p[0] for p in parts], axis=1) if n > 1 else parts[0][0]
        los = jnp.concatenate([p[1] for p in parts], axis=1) if n > 1 else parts[0][1]
        cum_all = _dot(tri2, jnp.concatenate([his, los], axis=0))
        cums = [cum_all[:, g * LANES:(g + 1) * LANES] for g in range(n)]
        by_chunk = lambda a: [a[g * c_len:(g + 1) * c_len] for g in range(n)]
        norms = by_chunk(jnp.sqrt(_dot(bf(jnp.concatenate([x["kk"] * x["kk"] for x in xs], axis=0)),
                                       ones)))
        hi, lo = _split_bf16(jnp.concatenate([x["r"] * x["k"] * r_k for x in xs], axis=0))
        for rw, x, wgt in zip(rows, xs, by_chunk(_dot(hi, ones) + _dot(lo, ones))):
            bonus_ref[rw, :] = wgt * x["v"]
        ps = each(prepare, xs, norms, cums)
        yield

        grams = [_dot_nt(p["lhs"], p["rhs"]) for p in ps]
        n_abs = [jnp.where(strict, g[:c_len, :two], 0.0) for g in grams]
        t1s = [_dot(bf(jnp.where(strict, g[:c_len, two:], 0.0)), p["v_st"])
               for g, p in zip(grams, ps)]
        b_rs = [bf(jnp.concatenate([jnp.where(incl, g[c_len:, :two], 0.0),
                                    jnp.where(incl, g[c_len:, two:], 0.0)], axis=1)) for g in grams]
        yield

        pws = [_dot(bf(x), stack(bf(x))) for x in n_abs]
        invs = [jnp.where(ident, 1.0, x) for x in n_abs]
        yield
        terms = 2
        while terms < c_len:
            last = 2 * terms >= c_len
            nxt_p, nxt_t = [], []
            for pw, inv in zip(pws, invs):
                pwb = bf(pw)
                if last:
                    nxt_t.append(inv + _dot(pwb, stack(bf(inv))))
                else:
                    both = _dot(pwb, jnp.concatenate([stack(pwb), stack(bf(inv))], axis=1))
                    nxt_p.append(both[:, :two])
                    nxt_t.append(inv + both[:, two:])
            pws, invs = nxt_p, nxt_t
            terms *= 2
            yield

        wus = [bf(_dot(bf(inv), jnp.concatenate([p["a_st"], stack(bf(t1))], axis=1)))
               for inv, p, t1 in zip(invs, ps, t1s)]
        yield
        zeros = jnp.zeros((c_len, LANES), BF16)
        tops = [_dot(p["tails_t"], jnp.concatenate(
                    [wu, jnp.concatenate([zeros, p["vb"]], axis=1)], axis=0))
                for p, wu in zip(ps, wus)]
        lows = [_dot(b_r, jnp.concatenate(
                    [jnp.concatenate([stack(wu[:, :LANES]), stack(wu[:, LANES:])], axis=1),
                     jnp.concatenate([jnp.zeros((two, LANES), BF16), p["v_st"]], axis=1)], axis=0))
                for b_r, p, wu in zip(b_rs, ps, wus)]
        for p, top, low in zip(ps, tops, lows):
            m_mat = p["decay"] + jnp.where(same_head, top[:, :LANES], 0.0)
            n_mat = jnp.where(same_head, top[:, LANES:], 0.0)
            q2 = p["r_t"] + low[:, :LANES]
            out.append((bf(jnp.concatenate([q2, m_mat], axis=0)), low[:, LANES:], n_mat))

    rows = [pl.ds(g * c_len, c_len) for g in range(n_chunks)]
    starts_seq = (step - 1) % tiles_per_seq == 0
    chain = {"state": jnp.where(starts_seq, 0.0, state_ref[...]), "next": 0}

    def chain_step():
        g = chain["next"]
        if g == n_chunks:
            return
        chain["next"] = g + 1
        res = _dot(qm_ref[g], bf(chain["state"]))
        y_ref[rows[g], :] = res[:c_len] + y3_ref[g]
        chain["state"] = res[c_len:] + n_ref[g]

    size = n_chunks // SCAN_GROUPS
    results = [[] for _ in range(SCAN_GROUPS)]
    waiting = [transitions(rows[i * size:(i + 1) * size], results[i]) for i in range(SCAN_GROUPS)]
    running, slot = [], 0
    while waiting or running:
        if waiting and slot % SCAN_SKEW == 0:
            running.append(waiting.pop(0))
        for gen in list(running):
            if next(gen, "done") == "done":
                running.remove(gen)
        chain_step()
        slot += 1
    while chain["next"] < n_chunks:
        chain_step()
    state_ref[...] = chain["state"]
    for g, (qm, y3, n_mat) in enumerate(sum(results, [])):
        qm_ref[g] = qm
        y3_ref[g] = y3
        n_ref[g] = n_mat


def _scan(r, k, v, lw, la, vecs):
    b, s, d = r.shape
    rows = min(SCAN_ROWS, s)
    assert s % rows == 0 and rows % (SCAN_GROUPS * SCAN_CHUNK) == 0
    pairs, tiles = d // LANES, s // rows
    n_tiles = b * pairs * tiles
    n_chunks = rows // SCAN_CHUNK

    def tile_of(j):
        return j // (pairs * tiles), j % tiles, (j // tiles) % pairs

    in_tile = pl.BlockSpec((None, rows, LANES), lambda i: tile_of(jnp.minimum(i, n_tiles - 1)))
    out_tile = pl.BlockSpec((None, rows, LANES), lambda i: tile_of(jnp.maximum(i - 1, 0)))
    vec_tile = pl.BlockSpec((vecs.shape[0], LANES),
                            lambda i: (0, tile_of(jnp.minimum(i, n_tiles - 1))[2]))
    out = jax.ShapeDtypeStruct((b, s, d), F32)
    return pl.pallas_call(
        functools.partial(_scan_kernel, tiles),
        out_shape=(out, out),
        grid=(n_tiles + 1,),
        in_specs=[in_tile] * 5 + [vec_tile],
        out_specs=(out_tile, in_tile),
        scratch_shapes=[pltpu.VMEM((LANES, LANES), F32),
                        pltpu.VMEM((n_chunks, SCAN_CHUNK + LANES, LANES), BF16),
                        pltpu.VMEM((n_chunks, SCAN_CHUNK, LANES), F32),
                        pltpu.VMEM((n_chunks, LANES, LANES), F32)],
        compiler_params=_params("arbitrary"),
        name="rwkv_scan",
    )(r, k, v, lw, la, vecs)


RW_OUT_TM = 512


def _rwkv_out_kernel(x_ref, y_ref, bonus_ref, gate_ref, lg_ref, lb_ref, wo_ref,
                     g_ref, wg_ref, wu_ref, wd_ref, o_ref):
    y = y_ref[...]
    ones = _head_ones()
    mu = _head_sum(y, ones, exact=True) * (1.0 / HEAD_DIM)
    dlt = y - mu
    var = _head_sum(dlt * dlt, ones, exact=False) * (1.0 / HEAD_DIM)
    yn = dlt * lax.rsqrt(var + GN_EPS) * lg_ref[...] + lb_ref[...] + bonus_ref[...]
    x = x_ref[...] + _dot((yn * gate_ref[...].astype(F32)).astype(BF16), wo_ref[...])
    o_ref[...] = _ffn_apply(x, g_ref, wg_ref, wu_ref, wd_ref)


def _rwkv_out(x2, y2, bonus2, gate2, lnx_g, lnx_b, wo, ffn):
    m, d = x2.shape
    tm = min(RW_OUT_TM, m)
    row = pl.BlockSpec((tm, d), lambda i: (i, 0))
    return pl.pallas_call(
        _rwkv_out_kernel,
        out_shape=jax.ShapeDtypeStruct((m, d), F32),
        grid=(m // tm,),
        in_specs=[row, row, row, row, _resident((1, d)), _resident((1, d)), _resident(wo.shape)]
        + _ffn_specs(ffn),
        out_specs=row,
        compiler_params=_params("parallel"),
        name="rwkv_out_ffn",
    )(x2, y2, bonus2, gate2, lnx_g.reshape(1, d), lnx_b.reshape(1, d), wo, *ffn)


def _attention_layer(x3, mix_g, rel_bias, w_in, q_norm, k_norm, w_out, ffn):
    b, s, d = x3.shape
    nat, streams = _proj(x3, mix_g, w_in, q_norm, k_norm)
    sb = _sb_attention(nat)
    outs, lses = [], []
    group_w = DIL_GROUP_HEADS * HEAD_DIM
    for g, r in enumerate(DILATIONS):
        if r == 1:
            src = nat.reshape(b, 1, s, -1)
            base = SB_COLS + GROUP_COLS * g
        else:
            src, base = streams.pop(0), 0
        offs = tuple((base + part * group_w) // LANES for part in range(3))
        bias = _bias_blocks(rel_bias[:, g * DIL_GROUP_HEADS:(g + 1) * DIL_GROUP_HEADS], r)
        o, l = _dil_attention(src, offs, bias)
        outs.append(o)
        lses.append(l)
    return _attn_out(x3, sb, outs, lses, w_out.astype(BF16), ffn)


def _pad_cols(w, n):
    return jnp.pad(w, ((0, 0), (0, n - w.shape[1])))


def _pad_rows(w, n):
    return jnp.pad(w, ((0, n - w.shape[0]), (0, 0)))


def _rwkv_layer(x3, mix_g, mix, w0, w1, w2, a0, a1, a2, g1, g2, k_k, k_a, r_k,
                w_r, w_k, w_v, w_o, lnx_g, lnx_b, ffn):
    b, s, d = x3.shape
    bf = lambda w: w.astype(BF16)
    lora = lambda w: -(-w // LANES) * LANES
    vecs = jnp.stack([w0, a0, k_k, k_a, r_k.reshape(d), jnp.zeros_like(w0),
                      jnp.zeros_like(w0), jnp.zeros_like(w0)], axis=0).astype(F32)
    dw, da, dg = lora(w1.shape[1]), lora(a1.shape[1]), lora(g1.shape[1])
    r, k, v, lw, la, gate = _rwkv_in(
        x3, mix_g, mix, bf(w_r), bf(w_k), bf(w_v),
        bf(_pad_cols(w1, dw)), bf(_pad_rows(w2, dw)),
        bf(_pad_cols(a1, da)), bf(_pad_rows(a2, da)),
        bf(_pad_cols(g1, dg)), bf(_pad_rows(g2, dg)))
    y, bonus = _scan(r, k, v, lw, la, vecs)
    flat = lambda t: t.reshape(b * s, d)
    out = _rwkv_out(flat(x3), flat(y), flat(bonus), flat(gate), lnx_g, lnx_b, bf(w_o), ffn)
    return out.reshape(b, s, d)


def kernel(x, ffn_norm, ffn_w_gate, ffn_w_up, ffn_w_down, mix_norm, rel_bias, attn_w_in, attn_q_norm, attn_k_norm, attn_w_out, rw_mix, rw_w0, rw_w1, rw_w2, rw_a0, rw_a1, rw_a2, rw_g1, rw_g2, rw_kk, rw_ka, rw_rk, rw_wr, rw_wk, rw_wv, rw_wo, rw_lnx_g, rw_lnx_b):
    b, s, d = x.shape
    depth = ffn_norm.shape[0]

    def ffn_operands(layer, half):
        return (ffn_norm[layer, half].reshape(1, d).astype(F32),
                ffn_w_gate[layer, half].astype(BF16), ffn_w_up[layer, half].astype(BF16),
                ffn_w_down[layer, half].astype(BF16))

    for layer in range(depth):
        x = _ffn(x.reshape(b * s, d), *ffn_operands(layer, 0)).reshape(b, s, d)
        second = ffn_operands(layer, 1)
        if layer % 2 == 0:
            e = layer // 2
            x = _attention_layer(x, mix_norm[layer], rel_bias, attn_w_in[e], attn_q_norm[e],
                                 attn_k_norm[e], attn_w_out[e], second)
        else:
            o = layer // 2
            x = _rwkv_layer(x, mix_norm[layer], rw_mix[o], rw_w0[o], rw_w1[o], rw_w2[o],
                            rw_a0[o], rw_a1[o], rw_a2[o], rw_g1[o], rw_g2[o], rw_kk[o],
                            rw_ka[o], rw_rk[o], rw_wr[o], rw_wk[o], rw_wv[o], rw_wo[o],
                            rw_lnx_g[o], rw_lnx_b[o], second)
    return x
```

```python
import functools
import math

import jax
import jax.numpy as jnp
from jax import lax
from jax.experimental import pallas as pl
from jax.experimental.pallas import tpu as pltpu

F32 = jnp.float32
BF16 = jnp.bfloat16

HEAD_DIM = 64
LANES = 128
SUBLANES = 8
SB_HEADS = 4
DIL_HEADS = 12
DIL_PATTERNS = ((128, 1), (512, 4), (2048, 16))
DIL_GROUP_HEADS = 4
ATT_BLOCK = 128
N_BUCKETS = 32
MAX_DISTANCE = 2048
NORM_EPS = 1e-6
GN_EPS = 64e-5
NEG_INF = -1e30
LOG2E = math.log2(math.e)
SCAN_CHUNK = 64
VMEM_LIMIT = 56 * 1024 * 1024


def _params(*sem):
    return pltpu.CompilerParams(dimension_semantics=sem, vmem_limit_bytes=VMEM_LIMIT)


def _resident(shape):
    nd = len(shape)
    return pl.BlockSpec(shape, lambda *_: (0,) * nd, pipeline_mode=pl.Buffered(1))


def _rms(x, g):
    ms = jnp.mean(x * x, axis=-1, keepdims=True)
    return x * lax.rsqrt(ms + NORM_EPS) * g


def _dot(a, b):
    return jnp.dot(a, b, preferred_element_type=F32)


def _dot_nt(a, b):
    return lax.dot_general(a, b, (((1,), (1,)), ((), ())), preferred_element_type=F32)


def _split_bf16(x):
    hi = x.astype(BF16)
    lo = (x - hi.astype(F32)).astype(BF16)
    return hi, lo


def _head_ones():
    r = lax.broadcasted_iota(jnp.int32, (LANES, LANES), 0) // HEAD_DIM
    c = lax.broadcasted_iota(jnp.int32, (LANES, LANES), 1) // HEAD_DIM
    return (r == c).astype(BF16)


def _head_sum(x, ones, exact):
    outs = []
    for c in range(x.shape[1] // LANES):
        xc = x[:, c * LANES:(c + 1) * LANES]
        if exact:
            hi, lo = _split_bf16(xc)
            outs.append(_dot(hi, ones) + _dot(lo, ones))
        else:
            outs.append(_dot(xc.astype(BF16), ones))
    return outs[0] if len(outs) == 1 else jnp.concatenate(outs, axis=1)


FFN_TM = 512
FFN_FK = 256


def _ffn_apply(x, g_ref, wg_ref, wu_ref, wd_ref):
    h = _rms(x, g_ref[...]).astype(BF16)
    d_ff = wg_ref.shape[1]
    acc = jnp.zeros(x.shape, F32)
    for c in range(d_ff // FFN_FK):
        sl = slice(c * FFN_FK, (c + 1) * FFN_FK)
        gate = _dot(h, wg_ref[:, sl])
        up = _dot(h, wu_ref[:, sl])
        act = (gate * jax.nn.sigmoid(gate) * up).astype(BF16)
        acc = acc + _dot(act, wd_ref[sl, :])
    return x + 0.5 * acc


def _ffn_specs(ffn):
    return [_resident(a.shape) for a in ffn]


def _ffn_kernel(x_ref, g_ref, wg_ref, wu_ref, wd_ref, o_ref):
    o_ref[...] = _ffn_apply(x_ref[...], g_ref, wg_ref, wu_ref, wd_ref)


def _ffn(x2, *ffn):
    m, d = x2.shape
    tm = min(FFN_TM, m)
    return pl.pallas_call(
        _ffn_kernel,
        out_shape=jax.ShapeDtypeStruct((m, d), F32),
        grid=(m // tm,),
        in_specs=[pl.BlockSpec((tm, d), lambda i: (i, 0))] + _ffn_specs(ffn),
        out_specs=pl.BlockSpec((tm, d), lambda i: (i, 0)),
        compiler_params=_params("parallel"),
        name="ffn",
    )(x2, *ffn)


PROJ_TM = 512
PROJ_NC = 256
SB_COLS = 3 * SB_HEADS * HEAD_DIM
DIL_COLS = DIL_HEADS * HEAD_DIM
GROUP_COLS = 3 * DIL_GROUP_HEADS * HEAD_DIM
DILATIONS = tuple(r for _, r in DIL_PATTERNS)
assert PROJ_NC == DIL_GROUP_HEADS * HEAD_DIM


def _proj_kernel(x_ref, g_ref, w_ref, qn_ref, kn_ref, nat_ref, *rest):
    stream_refs, pt_ref = rest[:-1], rest[-1]
    h = _rms(x_ref[...], g_ref[...]).astype(BF16)
    tm = x_ref.shape[0]
    ones = _head_ones()
    scale = HEAD_DIM ** -0.5
    streams = dict(zip([g for g, r in enumerate(DILATIONS) if r > 1], stream_refs))
    n_chunks = w_ref.shape[1] // PROJ_NC
    cols = lambda c: slice(c * PROJ_NC, (c + 1) * PROJ_NC)
    group_of = lambda c: c // 3 - 1

    def finish(c, p):
        part, grp = c % 3, group_of(c)
        if grp >= 0 and part < 2:
            norm_gain = qn_ref[...] if part == 0 else kn_ref[...]
            ms = _head_sum(p * p, ones, exact=False) * (1.0 / HEAD_DIM)
            p = p * lax.rsqrt(ms + NORM_EPS) * norm_gain
        if part == 0:
            p = p * (scale * LOG2E if grp < 0 else scale)
        if grp not in streams:
            nat_ref[:, cols(c)] = p.astype(BF16)
            return
        r = DILATIONS[grp]
        n = tm // r
        for j in range(PROJ_NC // LANES):
            buf = pt_ref.at[c % 2, j]
            buf[...] = p[:, j * LANES:(j + 1) * LANES]
            lanes = slice(part * PROJ_NC + j * LANES, part * PROJ_NC + (j + 1) * LANES)
            for cc in range(r):
                streams[grp][cc, :, lanes] = buf[pl.ds(cc, n, stride=r), :].astype(BF16)

    product = lambda c: _dot(h, w_ref[:, cols(c)])
    pending = product(0)
    for c in range(1, n_chunks):
        nxt = product(c)
        finish(c - 1, pending)
        pending = nxt
    finish(n_chunks - 1, pending)


def _proj(x3, g, w_in, q_norm, k_norm):
    b, s, d = x3.shape
    tm = min(PROJ_TM, s)
    tiles = s // tm
    assert all(DILATIONS[g] == 1 for g in range(len(DILATIONS)) if g < DILATIONS.count(1))
    group_w = DIL_GROUP_HEADS * HEAD_DIM
    pieces = [w_in[:, :SB_COLS]] + [
        w_in[:, SB_COLS + part * DIL_COLS + g * group_w:][:, :group_w]
        for g in range(len(DILATIONS)) for part in range(3)]
    w = jnp.concatenate(pieces, axis=1).astype(BF16)
    nat_w = SB_COLS + GROUP_COLS * DILATIONS.count(1)
    tile = lambda v: jnp.tile(v.astype(F32), PROJ_NC // HEAD_DIM).reshape(1, PROJ_NC)
    stream_rs = [r for r in DILATIONS if r > 1]
    outs = pl.pallas_call(
        _proj_kernel,
        out_shape=(jax.ShapeDtypeStruct((b, s, nat_w), BF16),)
        + tuple(jax.ShapeDtypeStruct((b, r, s // r, GROUP_COLS), BF16) for r in stream_rs),
        grid=(b, tiles),
        in_specs=[
            pl.BlockSpec((None, tm, d), lambda bi, t: (bi, t, 0)),
            _resident((1, d)),
            _resident(w.shape),
            _resident((1, PROJ_NC)),
            _resident((1, PROJ_NC)),
        ],
        out_specs=(pl.BlockSpec((None, tm, nat_w), lambda bi, t: (bi, t, 0)),)
        + tuple(pl.BlockSpec((None, r, tm // r, GROUP_COLS), lambda bi, t: (bi, 0, t, 0))
                for r in stream_rs),
        scratch_shapes=[pltpu.VMEM((2, PROJ_NC // LANES, tm, LANES), F32)],
        compiler_params=_params("parallel", "parallel"),
        name="attn_proj",
    )(x3, g.reshape(1, d), w, tile(q_norm), tile(k_norm))
    return outs[0], list(outs[1:])


SB_TQ = 512
SB_UNROLL = 8


def _sb_kernel(q_ref, k_ref, v_ref, o_ref, qh_ref, acc_ref, run_ref):
    qi = pl.program_id(2)
    tq = q_ref.shape[0]
    tk = ATT_BLOCK
    nsub = tq // tk
    lane = lax.broadcasted_iota(jnp.int32, (1, LANES), 1)
    head0 = lane < HEAD_DIM
    q = q_ref[...]
    zero = jnp.zeros_like(q)
    qh_ref[0] = jnp.where(head0, q, zero)
    qh_ref[1] = jnp.where(head0, zero, q)
    acc_ref[...] = jnp.zeros_like(acc_ref)
    run_ref[...] = jnp.zeros_like(run_ref)
    later = (lax.broadcasted_iota(jnp.int32, (tk, tk), 0)
             > lax.broadcasted_iota(jnp.int32, (tk, tk), 1)).astype(BF16)

    def step(blocks, masked):
        starts = [pl.multiple_of(j * tk, tk) for j, _ in blocks]
        kbs = [k_ref[pl.ds(st, tk), :] for st in starts]
        vbs = [v_ref[pl.ds(st, tk), :] for st in starts]
        chains = [(b, h) for b in range(len(blocks)) for h in range(2)]
        zs = [_dot_nt(qh_ref[h, blocks[b][1]:, :], kbs[b]) for b, h in chains]
        yield
        log_betas = [jnp.minimum(z, 0.0) - jnp.log(1.0 + jnp.exp2(-jnp.abs(z))) * LOG2E for z in zs]
        log_keeps = [lb - z for lb, z in zip(log_betas, zs)]
        if masked:
            stricts = [lax.broadcasted_iota(jnp.int32, z.shape, 1)
                       < lax.broadcasted_iota(jnp.int32, z.shape, 0) for z in zs]
            log_keeps = [jnp.where(s, lk, 0.0) for s, lk in zip(stricts, log_keeps)]
        afters = [_dot(lk.astype(BF16), later) for lk in log_keeps]
        totals = [jnp.sum(lk, axis=1, keepdims=True) for lk in log_keeps]
        yield
        for h in range(2):
            for b, (_, r0) in enumerate(blocks):
                c = 2 * b + h
                run = run_ref[h, r0:, :]
                w = jnp.exp2(log_betas[c] + afters[c] + run)
                if masked:
                    w = jnp.where(stricts[c], w, 0.0)
                run_ref[h, r0:, :] = run + totals[c]
                acc_ref[h, r0:, :] += _dot(w.astype(BF16), vbs[b])

    def run_skewed(gens):
        waiting, running = list(gens), []
        while waiting or running:
            for _ in range(2):
                if waiting:
                    running.append(waiting.pop(0))
            for gen in list(running):
                if next(gen, "done") == "done":
                    running.remove(gen)

    run_skewed([step([(qi * nsub + c, c * tk) for c in reversed(range(nsub))], True)])

    def trip(j, n_blocks):
        run_skewed([step([(j - u, 0)], False) for u in range(n_blocks)])

    assert SB_UNROLL == 2 * nsub
    odd = qi % 2

    @pl.when(odd == 1)
    def _():
        trip(qi * nsub - 1, nsub)

    def body(i, carry):
        trip(qi * nsub - 1 - nsub * odd - SB_UNROLL * i, SB_UNROLL)
        return carry

    lax.fori_loop(0, qi // 2, body, 0)
    o_ref[...] = jnp.where(head0, acc_ref[0], acc_ref[1]).astype(o_ref.dtype)


def _sb_attention(qkv):
    b, s, _ = qkv.shape
    pairs = SB_HEADS * HEAD_DIM // LANES
    tq = min(SB_TQ, s)
    return pl.pallas_call(
        _sb_kernel,
        out_shape=jax.ShapeDtypeStruct((b, s, SB_HEADS * HEAD_DIM), BF16),
        grid=(b, pairs, s // tq),
        in_specs=[
            pl.BlockSpec((None, tq, LANES), lambda bi, p, i: (bi, i, p)),
            pl.BlockSpec((None, s, LANES), lambda bi, p, i: (bi, 0, pairs + p)),
            pl.BlockSpec((None, s, LANES), lambda bi, p, i: (bi, 0, 2 * pairs + p)),
        ],
        out_specs=pl.BlockSpec((None, tq, LANES), lambda bi, p, i: (bi, i, p)),
        scratch_shapes=[pltpu.VMEM((2, tq, LANES), BF16), pltpu.VMEM((2, tq, LANES), F32),
                        pltpu.VMEM((2, tq, LANES), F32)],
        compiler_params=_params("parallel", "parallel", "arbitrary"),
        name="sb_attn",
    )(qkv, qkv, qkv)


def _t5_bucket(dist):
    max_exact = N_BUCKETS // 2
    d = jnp.maximum(dist, 1).astype(F32)
    large = max_exact + (jnp.log(d / max_exact) / math.log(MAX_DISTANCE / max_exact)
                         * (N_BUCKETS - max_exact)).astype(jnp.int32)
    large = jnp.minimum(large, N_BUCKETS - 1)
    return jnp.where(dist < max_exact, dist, large)


def _bias_blocks(rel_bias_group, dilation):
    qi = jnp.arange(ATT_BLOCK)[:, None]
    kj = jnp.arange(2 * ATT_BLOCK)[None, :] - ATT_BLOCK
    dist = qi - kj
    bucket = _t5_bucket(jnp.maximum(dist, 0) * dilation)
    onehot = (bucket[None] == jnp.arange(N_BUCKETS)[:, None, None]).astype(F32)
    bias = jnp.einsum('nh,nqk->hqk', rel_bias_group.astype(F32), onehot,
                      precision=lax.Precision.HIGHEST)
    return bias.reshape(2, 2, ATT_BLOCK, 2 * ATT_BLOCK)


DIL_ROWS = 512


def _dil_kernel(q_ref, kp_ref, kc_ref, vp_ref, vc_ref, bias_ref, o_ref, lse_ref):
    n = pl.program_id(2)
    blk = ATT_BLOCK
    nb = q_ref.shape[0] // blk
    pairs = q_ref.shape[1] // LANES
    lane = lax.broadcasted_iota(jnp.int32, (1, LANES), 1)
    head0 = lane < HEAD_DIM
    row = lax.broadcasted_iota(jnp.int32, (blk, 2 * blk), 0)
    col = lax.broadcasted_iota(jnp.int32, (blk, 2 * blk), 1)
    dist = row - col + blk
    window = (dist >= 0) & (dist <= blk)
    first = window & ((n > 0) | (col >= blk))
    keys = jnp.concatenate([kp_ref[...], kc_ref[...]], axis=0)
    vals = jnp.concatenate([vp_ref[...], vc_ref[...]], axis=0)
    q = q_ref[...]
    zero = jnp.zeros_like(q)
    first_heads = jnp.tile(head0, (1, pairs))
    qh = (jnp.where(first_heads, q, zero), jnp.where(first_heads, zero, q))
    chains = [(i, p, h) for i in range(nb) for p in range(pairs) for h in range(2)]
    span = lambda i: slice(i * blk, (i + 2) * blk)
    lanes = lambda p: slice(p * LANES, (p + 1) * LANES)
    zs = [_dot_nt(qh[h][i * blk:(i + 1) * blk, lanes(p)], keys[span(i), lanes(p)])
          for i, p, h in chains]
    logits = [jnp.where(first if i == 0 else window, z + bias_ref[p, h], NEG_INF)
              for (i, p, h), z in zip(chains, zs)]
    ms = [jnp.max(l, axis=1, keepdims=True) for l in logits]
    ps = [jnp.exp(l - m) for l, m in zip(logits, ms)]
    dens = [jnp.sum(p_, axis=1, keepdims=True) for p_ in ps]
    outs = [_dot((p_ / den).astype(BF16), vals[span(i), lanes(p)])
            for (i, p, _), p_, den in zip(chains, ps, dens)]
    lses = [m + jnp.log(den) for m, den in zip(ms, dens)]
    for c in range(0, len(chains), 2):
        i, p, _ = chains[c]
        rows = slice(i * blk, (i + 1) * blk)
        o_ref[rows, lanes(p)] = jnp.where(head0, outs[c], outs[c + 1])
        lse_ref[rows, lanes(p)] = jnp.where(head0, lses[c], lses[c + 1])


def _dil_attention(src, col_blocks, bias):
    b, r, l, _ = src.shape
    width = DIL_GROUP_HEADS * HEAD_DIM
    blk = ATT_BLOCK
    rows = min(DIL_ROWS, l)
    nb = rows // blk
    assert all(off * LANES % width == 0 for off in col_blocks)
    qo, ko, vo = (off * LANES // width for off in col_blocks)
    cur = lambda off: (lambda bi, c, n: (bi, c, n, off))
    prev = lambda off: (lambda bi, c, n: (bi, c, jnp.maximum(n * nb - 1, 0), off))
    tile = (None, None, rows, width)
    one = (None, None, blk, width)
    out = jax.ShapeDtypeStruct((b, r, l, width), F32)
    return pl.pallas_call(
        _dil_kernel,
        out_shape=(out, out),
        grid=(b, r, l // rows),
        in_specs=[
            pl.BlockSpec(tile, cur(qo)),
            pl.BlockSpec(one, prev(ko)),
            pl.BlockSpec(tile, cur(ko)),
            pl.BlockSpec(one, prev(vo)),
            pl.BlockSpec(tile, cur(vo)),
            _resident(bias.shape),
        ],
        out_specs=(pl.BlockSpec(tile, cur(0)), pl.BlockSpec(tile, cur(0))),
        compiler_params=_params("parallel", "parallel", "arbitrary"),
        name="dil_attn",
    )(src, src, src, src, src, bias)


OUT_TM = 512


def _attn_out_kernel(x_ref, sb_ref, o0_ref, o1_ref, o2_ref, l0_ref, l1_ref, l2_ref, w_ref,
                     g_ref, wg_ref, wu_ref, wd_ref, y_ref, *order_refs):
    tm = x_ref.shape[0]
    scratch = list(order_refs)

    def natural(ref):
        r = ref.shape[0]
        if r == 1:
            return ref[0]
        buf = scratch.pop()
        for c in range(r):
            for j in range(buf.shape[0]):
                buf[j, pl.ds(c, tm // r, stride=r), :] = ref[c, :, j * LANES:(j + 1) * LANES]
        return jnp.concatenate([buf[j] for j in range(buf.shape[0])], axis=1)

    l0, l1, l2 = natural(l0_ref), natural(l1_ref), natural(l2_ref)
    mx = jnp.maximum(jnp.maximum(l0, l1), l2)
    e0, e1, e2 = jnp.exp(l0 - mx), jnp.exp(l1 - mx), jnp.exp(l2 - mx)
    den = e0 + e1 + e2
    out_b = ((e0 / den) * natural(o0_ref) + (e1 / den) * natural(o1_ref)
             + (e2 / den) * natural(o2_ref))
    na = sb_ref.shape[1]
    y = _dot(sb_ref[...], w_ref[:na, :]) + _dot(out_b.astype(BF16), w_ref[na:, :])
    y_ref[...] = _ffn_apply(x_ref[...] + y, g_ref, wg_ref, wu_ref, wd_ref)


def _attn_out(x3, sb, outs, lses, w_out, ffn):
    b, s, d = x3.shape
    tm = min(OUT_TM, s)
    wide = sb.shape[2]
    row = lambda w: pl.BlockSpec((None, tm, w), lambda bi, t: (bi, t, 0))
    stream = lambda a: pl.BlockSpec((None, a.shape[1], tm // a.shape[1], wide),
                                    lambda bi, t: (bi, 0, t, 0))
    n_buffers = 2 * sum(1 for a in outs if a.shape[1] > 1)
    return pl.pallas_call(
        _attn_out_kernel,
        out_shape=jax.ShapeDtypeStruct((b, s, d), F32),
        grid=(b, s // tm),
        in_specs=[row(d), row(wide)] + [stream(a) for a in (*outs, *lses)]
        + [_resident(w_out.shape)] + _ffn_specs(ffn),
        out_specs=row(d),
        scratch_shapes=[pltpu.VMEM((wide // LANES, tm, LANES), F32)] * n_buffers,
        compiler_params=_params("parallel", "parallel"),
        name="attn_out_ffn",
    )(x3, sb, *outs, *lses, w_out, *ffn)


RW_TS = 512


def _rwkv_in_kernel(x_ref, xp_ref, g_ref, mix_ref, wr_ref, wk_ref, wv_ref,
                    w1_ref, w2_ref, a1_ref, a2_ref, g1_ref, g2_ref,
                    r_ref, k_ref, v_ref, lw_ref, la_ref, gate_ref):
    si = pl.program_id(1)
    gain = g_ref[...]
    h = _rms(x_ref[...], gain)
    prev_last = _rms(xp_ref[...], gain)[SUBLANES - 1:, :] * (si > 0).astype(F32)
    rows = lax.broadcasted_iota(jnp.int32, h.shape, 0)
    shifted = jnp.where(rows == 0, prev_last, pltpu.roll(h, 1, axis=0))
    xx = shifted - h
    mixed = lambda i: (h + xx * mix_ref[i:i + 1, :]).astype(BF16)

    r_ref[...] = _dot(mixed(0), wr_ref[...]).astype(r_ref.dtype)
    k_ref[...] = _dot(mixed(2), wk_ref[...]).astype(k_ref.dtype)
    v_ref[...] = _dot(mixed(3), wv_ref[...]).astype(v_ref.dtype)
    lw_ref[...] = _dot(jnp.tanh(_dot(mixed(1), w1_ref[...])).astype(BF16), w2_ref[...])
    la_ref[...] = _dot(_dot(mixed(4), a1_ref[...]).astype(BF16), a2_ref[...]).astype(la_ref.dtype)
    gate_ref[...] = _dot(jax.nn.sigmoid(_dot(mixed(5), g1_ref[...])).astype(BF16),
                         g2_ref[...]).astype(gate_ref.dtype)


def _rwkv_in(x3, g, mix, wr, wk, wv, w1, w2, a1, a2, g1, g2):
    b, s, d = x3.shape
    ts = min(RW_TS, s)
    tile = pl.BlockSpec((None, ts, d), lambda bi, si: (bi, si, 0))
    prev = pl.BlockSpec((None, SUBLANES, d),
                        lambda bi, si: (bi, jnp.maximum(si * (ts // SUBLANES) - 1, 0), 0))
    out = lambda dt: jax.ShapeDtypeStruct((b, s, d), dt)
    weights = (wr, wk, wv, w1, w2, a1, a2, g1, g2)
    return pl.pallas_call(
        _rwkv_in_kernel,
        out_shape=(out(BF16), out(BF16), out(BF16), out(F32), out(BF16), out(BF16)),
        grid=(b, s // ts),
        in_specs=[tile, prev, _resident((1, d)), _resident(mix.shape)]
        + [_resident(w.shape) for w in weights],
        out_specs=(tile,) * 6,
        compiler_params=_params("parallel", "arbitrary"),
        name="rwkv_in",
    )(x3, x3, g.reshape(1, d), mix, *weights)


SCAN_ROWS = 1024
SCAN_GROUPS = 2
SCAN_SKEW = 2


def _scan_kernel(tiles_per_seq, r_ref, k_ref, v_ref, lw_ref, la_ref, vec_ref, y_ref, bonus_ref,
                 state_ref, qm_ref, y3_ref, n_ref):
    c_len = SCAN_CHUNK
    two = 2 * c_len
    n_chunks = r_ref.shape[0] // c_len
    step = pl.program_id(0)

    @pl.when(step == 0)
    def _():
        state_ref[...] = jnp.zeros_like(state_ref)
        qm_ref[...] = jnp.zeros_like(qm_ref)
        y3_ref[...] = jnp.zeros_like(y3_ref)
        n_ref[...] = jnp.zeros_like(n_ref)

    lane = lax.broadcasted_iota(jnp.int32, (1, LANES), 1)
    head0 = lane < HEAD_DIM
    ti = lax.broadcasted_iota(jnp.int32, (c_len, two), 0)
    ii = lax.broadcasted_iota(jnp.int32, (c_len, two), 1) % c_len
    strict, incl, ident = ti > ii, ti >= ii, ti == ii
    ri = lax.broadcasted_iota(jnp.int32, (LANES, LANES), 0)
    ci = lax.broadcasted_iota(jnp.int32, (LANES, LANES), 1)
    same_head = (ri // HEAD_DIM) == (ci // HEAD_DIM)
    eye = ri == ci
    tri = (lax.broadcasted_iota(jnp.int32, (c_len, c_len), 0)
           >= lax.broadcasted_iota(jnp.int32, (c_len, c_len), 1)).astype(BF16)
    tri2 = jnp.concatenate([tri, tri], axis=1)

    def stack(x):
        zero = jnp.zeros_like(x)
        return jnp.concatenate([jnp.where(head0, x, zero), jnp.where(head0, zero, x)], axis=0)

    bf = lambda x: x.astype(BF16)
    each = lambda f, *ls: [f(*xs) for xs in zip(*ls)]

    w0, a0, k_k, k_a, r_k = (vec_ref[i:i + 1, :] for i in range(5))
    ones = _head_ones()

    def token_maps(rw):
        r, k, v, la = (ref[rw, :].astype(F32) for ref in (r_ref, k_ref, v_ref, la_ref))
        t = -(w0 + lw_ref[rw, :])
        w_log = -(jnp.maximum(t, 0.0) + jnp.log(1.0 + jnp.exp(-jnp.abs(t)))) - 0.5
        a = jax.nn.sigmoid(a0 + la)
        return dict(r=r, v=v, a=a, kk=k * k_k, k=k * (1.0 + (a - 1.0) * k_a),
                    ld=-jnp.exp(w_log))

    def prepare(x, norm, cum):
        r, k, v, ld = x["r"], x["k"], x["v"], x["ld"]
        kk = x["kk"] / jnp.maximum(norm, 1e-12)
        na, bb = -kk, kk * x["a"]
        total = cum[c_len - 1:, :]
        e_neg = jnp.exp(-cum)
        e_tail = jnp.exp(total - cum)
        a_t = bf(na * jnp.exp(cum - ld))
        r_t = r * jnp.exp(cum)
        vb = bf(v)
        tails = jnp.concatenate([bb * e_tail, k * e_tail], axis=0)
        return dict(
            r_t=r_t, a_st=stack(a_t), vb=vb, v_st=stack(vb),
            lhs=jnp.concatenate([a_t, bf(r_t)], axis=0),
            rhs=jnp.concatenate([stack(bf(bb * e_neg)), stack(bf(k * e_neg))], axis=0),
            tails_t=bf(jnp.transpose(tails)),
            decay=jnp.where(eye, jnp.exp(total), 0.0))

    def transitions(rows, out):
        n = len(rows)
        xs = []
        for g, rw in enumerate(rows):
            xs.append(token_maps(rw))
            if g % 2:
                yield
        parts = [_split_bf16(x["ld"]) for x in xs]
        his = jnp.concatenate([p[0] for p in parts], axis=1) if n > 1 else parts[0][0]
        los = jnp.concatenate([p[1] for p in parts], axis=1) if n > 1 else parts[0][1]
        cum_all = _dot(tri2, jnp.concatenate([his, los], axis=0))
        cums = [cum_all[:, g * LANES:(g + 1) * LANES] for g in range(n)]
        by_chunk = lambda a: [a[g * c_len:(g + 1) * c_len] for g in range(n)]
        norms = by_chunk(jnp.sqrt(_dot(bf(jnp.concatenate([x["kk"] * x["kk"] for x in xs], axis=0)),
                                       ones)))
        hi, lo = _split_bf16(jnp.concatenate([x["r"] * x["k"] * r_k for x in xs], axis=0))
        for rw, x, wgt in zip(rows, xs, by_chunk(_dot(hi, ones) + _dot(lo, ones))):
            bonus_ref[rw, :] = wgt * x["v"]
        ps = each(prepare, xs, norms, cums)
        yield

        grams = [_dot_nt(p["lhs"], p["rhs"]) for p in ps]
        n_abs = [jnp.where(strict, g[:c_len, :two], 0.0) for g in grams]
        t1s = [_dot(bf(jnp.where(strict, g[:c_len, two:], 0.0)), p["v_st"])
               for g, p in zip(grams, ps)]
        b_rs = [bf(jnp.concatenate([jnp.where(incl, g[c_len:, :two], 0.0),
                                    jnp.where(incl, g[c_len:, two:], 0.0)], axis=1)) for g in grams]
        yield

        pws = [_dot(bf(x), stack(bf(x))) for x in n_abs]
        invs = [jnp.where(ident, 1.0, x) for x in n_abs]
        yield
        terms = 2
        while terms < c_len:
            last = 2 * terms >= c_len
            nxt_p, nxt_t = [], []
            for pw, inv in zip(pws, invs):
                pwb = bf(pw)
                if last:
                    nxt_t.append(inv + _dot(pwb, stack(bf(inv))))
                else:
                    both = _dot(pwb, jnp.concatenate([stack(pwb), stack(bf(inv))], axis=1))
                    nxt_p.append(both[:, :two])
                    nxt_t.append(inv + both[:, two:])
            pws, invs = nxt_p, nxt_t
            terms *= 2
            yield

        wus = [bf(_dot(bf(inv), jnp.concatenate([p["a_st"], stack(bf(t1))], axis=1)))
               for inv, p, t1 in zip(invs, ps, t1s)]
        yield
        zeros = jnp.zeros((c_len, LANES), BF16)
        tops = [_dot(p["tails_t"], jnp.concatenate(
                    [wu, jnp.concatenate([zeros, p["vb"]], axis=1)], axis=0))
                for p, wu in zip(ps, wus)]
        lows = [_dot(b_r, jnp.concatenate(
                    [jnp.concatenate([stack(wu[:, :LANES]), stack(wu[:, LANES:])], axis=1),
                     jnp.concatenate([jnp.zeros((two, LANES), BF16), p["v_st"]], axis=1)], axis=0))
                for b_r, p, wu in zip(b_rs, ps, wus)]
        for p, top, low in zip(ps, tops, lows):
            m_mat = p["decay"] + jnp.where(same_head, top[:, :LANES], 0.0)
            n_mat = jnp.where(same_head, top[:, LANES:], 0.0)
            q2 = p["r_t"] + low[:, :LANES]
            out.append((bf(jnp.concatenate([q2, m_mat], axis=0)), low[:, LANES:], n_mat))

    rows = [pl.ds(g * c_len, c_len) for g in range(n_chunks)]
    starts_seq = (step - 1) % tiles_per_seq == 0
    chain = {"state": jnp.where(starts_seq, 0.0, state_ref[...]), "next": 0}

    def chain_step():
        g = chain["next"]
        if g == n_chunks:
            return
        chain["next"] = g + 1
        res = _dot(qm_ref[g], bf(chain["state"]))
        y_ref[rows[g], :] = res[:c_len] + y3_ref[g]
        chain["state"] = res[c_len:] + n_ref[g]

    size = n_chunks // SCAN_GROUPS
    results = [[] for _ in range(SCAN_GROUPS)]
    waiting = [transitions(rows[i * size:(i + 1) * size], results[i]) for i in range(SCAN_GROUPS)]
    running, slot = [], 0
    while waiting or running:
        if waiting and slot % SCAN_SKEW == 0:
            running.append(waiting.pop(0))
        for gen in list(running):
            if next(gen, "done") == "done":
                running.remove(gen)
        chain_step()
        slot += 1
    while chain["next"] < n_chunks:
        chain_step()
    state_ref[...] = chain["state"]
    for g, (qm, y3, n_mat) in enumerate(sum(results, [])):
        qm_ref[g] = qm
        y3_ref[g] = y3
        n_ref[g] = n_mat


def _scan(r, k, v, lw, la, vecs):
    b, s, d = r.shape
    rows = min(SCAN_ROWS, s)
    assert s % rows == 0 and rows % (SCAN_GROUPS * SCAN_CHUNK) == 0
    pairs, tiles = d // LANES, s // rows
    n_tiles = b * pairs * tiles
    n_chunks = rows // SCAN_CHUNK

    def tile_of(j):
        return j // (pairs * tiles), j % tiles, (j // tiles) % pairs

    in_tile = pl.BlockSpec((None, rows, LANES), lambda i: tile_of(jnp.minimum(i, n_tiles - 1)))
    out_tile = pl.BlockSpec((None, rows, LANES), lambda i: tile_of(jnp.maximum(i - 1, 0)))
    vec_tile = pl.BlockSpec((vecs.shape[0], LANES),
                            lambda i: (0, tile_of(jnp.minimum(i, n_tiles - 1))[2]))
    out = jax.ShapeDtypeStruct((b, s, d), F32)
    return pl.pallas_call(
        functools.partial(_scan_kernel, tiles),
        out_shape=(out, out),
        grid=(n_tiles + 1,),
        in_specs=[in_tile] * 5 + [vec_tile],
        out_specs=(out_tile, in_tile),
        scratch_shapes=[pltpu.VMEM((LANES, LANES), F32),
                        pltpu.VMEM((n_chunks, SCAN_CHUNK + LANES, LANES), BF16),
                        pltpu.VMEM((n_chunks, SCAN_CHUNK, LANES), F32),
                        pltpu.VMEM((n_chunks, LANES, LANES), F32)],
        compiler_params=_params("arbitrary"),
        name="rwkv_scan",
    )(r, k, v, lw, la, vecs)


RW_OUT_TM = 512


def _rwkv_out_kernel(x_ref, y_ref, bonus_ref, gate_ref, lg_ref, lb_ref, wo_ref,
                     g_ref, wg_ref, wu_ref, wd_ref, o_ref):
    y = y_ref[...]
    ones = _head_ones()
    mu = _head_sum(y, ones, exact=True) * (1.0 / HEAD_DIM)
    dlt = y - mu
    var = _head_sum(dlt * dlt, ones, exact=False) * (1.0 / HEAD_DIM)
    yn = dlt * lax.rsqrt(var + GN_EPS) * lg_ref[...] + lb_ref[...] + bonus_ref[...]
    x = x_ref[...] + _dot((yn * gate_ref[...].astype(F32)).astype(BF16), wo_ref[...])
    o_ref[...] = _ffn_apply(x, g_ref, wg_ref, wu_ref, wd_ref)


def _rwkv_out(x2, y2, bonus2, gate2, lnx_g, lnx_b, wo, ffn):
    m, d = x2.shape
    tm = min(RW_OUT_TM, m)
    row = pl.BlockSpec((tm, d), lambda i: (i, 0))
    return pl.pallas_call(
        _rwkv_out_kernel,
        out_shape=jax.ShapeDtypeStruct((m, d), F32),
        grid=(m // tm,),
        in_specs=[row, row, row, row, _resident((1, d)), _resident((1, d)), _resident(wo.shape)]
        + _ffn_specs(ffn),
        out_specs=row,
        compiler_params=_params("parallel"),
        name="rwkv_out_ffn",
    )(x2, y2, bonus2, gate2, lnx_g.reshape(1, d), lnx_b.reshape(1, d), wo, *ffn)


def _attention_layer(x3, mix_g, rel_bias, w_in, q_norm, k_norm, w_out, ffn):
    b, s, d = x3.shape
    nat, streams = _proj(x3, mix_g, w_in, q_norm, k_norm)
    sb = _sb_attention(nat)
    outs, lses = [], []
    group_w = DIL_GROUP_HEADS * HEAD_DIM
    for g, r in enumerate(DILATIONS):
        if r == 1:
            src = nat.reshape(b, 1, s, -1)
            base = SB_COLS + GROUP_COLS * g
        else:
            src, base = streams.pop(0), 0
        offs = tuple((base + part * group_w) // LANES for part in range(3))
        bias = _bias_blocks(rel_bias[:, g * DIL_GROUP_HEADS:(g + 1) * DIL_GROUP_HEADS], r)
        o, l = _dil_attention(src, offs, bias)
        outs.append(o)
        lses.append(l)
    return _attn_out(x3, sb, outs, lses, w_out.astype(BF16), ffn)


def _pad_cols(w, n):
    return jnp.pad(w, ((0, 0), (0, n - w.shape[1])))


def _pad_rows(w, n):
    return jnp.pad(w, ((0, n - w.shape[0]), (0, 0)))


def _rwkv_layer(x3, mix_g, mix, w0, w1, w2, a0, a1, a2, g1, g2, k_k, k_a, r_k,
                w_r, w_k, w_v, w_o, lnx_g, lnx_b, ffn):
    b, s, d = x3.shape
    bf = lambda w: w.astype(BF16)
    lora = lambda w: -(-w // LANES) * LANES
    vecs = jnp.stack([w0, a0, k_k, k_a, r_k.reshape(d), jnp.zeros_like(w0),
                      jnp.zeros_like(w0), jnp.zeros_like(w0)], axis=0).astype(F32)
    dw, da, dg = lora(w1.shape[1]), lora(a1.shape[1]), lora(g1.shape[1])
    r, k, v, lw, la, gate = _rwkv_in(
        x3, mix_g, mix, bf(w_r), bf(w_k), bf(w_v),
        bf(_pad_cols(w1, dw)), bf(_pad_rows(w2, dw)),
        bf(_pad_cols(a1, da)), bf(_pad_rows(a2, da)),
        bf(_pad_cols(g1, dg)), bf(_pad_rows(g2, dg)))
    y, bonus = _scan(r, k, v, lw, la, vecs)
    flat = lambda t: t.reshape(b * s, d)
    out = _rwkv_out(flat(x3), flat(y), flat(bonus), flat(gate), lnx_g, lnx_b, bf(w_o), ffn)
    return out.reshape(b, s, d)


def kernel(x, ffn_norm, ffn_w_gate, ffn_w_up, ffn_w_down, mix_norm, rel_bias, attn_w_in, attn_q_norm, attn_k_norm, attn_w_out, rw_mix, rw_w0, rw_w1, rw_w2, rw_a0, rw_a1, rw_a2, rw_g1, rw_g2, rw_kk, rw_ka, rw_rk, rw_wr, rw_wk, rw_wv, rw_wo, rw_lnx_g, rw_lnx_b):
    b, s, d = x.shape
    depth = ffn_norm.shape[0]

    def ffn_operands(layer, half):
        return (ffn_norm[layer, half].reshape(1, d).astype(F32),
                ffn_w_gate[layer, half].astype(BF16), ffn_w_up[layer, half].astype(BF16),
                ffn_w_down[layer, half].astype(BF16))

    for layer in range(depth):
        x = _ffn(x.reshape(b * s, d), *ffn_operands(layer, 0)).reshape(b, s, d)
        second = ffn_operands(layer, 1)
        if layer % 2 == 0:
            e = layer // 2
            x = _attention_layer(x, mix_norm[layer], rel_bias, attn_w_in[e], attn_q_norm[e],
                                 attn_k_norm[e], attn_w_out[e], second)
        else:
            o = layer // 2
            x = _rwkv_layer(x, mix_norm[layer], rw_mix[o], rw_w0[o], rw_w1[o], rw_w2[o],
                            rw_a0[o], rw_a1[o], rw_a2[o], rw_g1[o], rw_g2[o], rw_kk[o],
                            rw_ka[o], rw_rk[o], rw_wr[o], rw_wk[o], rw_wv[o], rw_wo[o],
                            rw_lnx_g[o], rw_lnx_b[o], second)
    return x
```
